```python
import math
import jax
import jax.numpy as jnp
from jax import lax
import numpy as np

D_MODEL = 2048
BATCH = 8
SEQ = 2048
DEPTH = 2

BRANCH_WIDTH = D_MODEL
N_BRANCH = 3
EPS = 1e-6
GMLP_CHUNK = 128
GMLP_GROUPS = 8
GMLP_GROUP_CH = BRANCH_WIDTH // GMLP_GROUPS
GLA_HEADS = 4
GLA_DK = (BRANCH_WIDTH // 2) // GLA_HEADS
GLA_DV = BRANCH_WIDTH // GLA_HEADS
GLA_RANK = 16
GLA_TAU = 16.0
GLA_CHUNK = 64
DIFF_HEADS = 8
DIFF_HEAD_DIM = BRANCH_WIDTH // (2 * DIFF_HEADS)
DIFF_V_DIM = 2 * DIFF_HEAD_DIM
ATTN_BLOCK = 128
REL_BUCKETS = 32
REL_MAX_DIST = 128
IN_SIZES = (BRANCH_WIDTH, BRANCH_WIDTH, BRANCH_WIDTH,
            GLA_HEADS * GLA_DK, GLA_HEADS * GLA_DK, GLA_HEADS * GLA_DV, BRANCH_WIDTH, 2 * GLA_RANK,
            2 * DIFF_HEADS * DIFF_HEAD_DIM, 2 * DIFF_HEADS * DIFF_HEAD_DIM, DIFF_HEADS * DIFF_V_DIM, BRANCH_WIDTH)
D_IN = sum(IN_SIZES)

kernel_name = "hybrid_gmlp_gla_diffattn_encoder"


def rms_norm(x, g):
    xf = x.astype(jnp.float32)
    y = xf * lax.rsqrt(jnp.mean(xf * xf, axis=-1, keepdims=True) + EPS)
    return (y * g.astype(jnp.float32)).astype(x.dtype)


def layer_norm(x, g, b):
    xf = x.astype(jnp.float32)
    mu = jnp.mean(xf, axis=-1, keepdims=True)
    var = jnp.mean(jnp.square(xf - mu), axis=-1, keepdims=True)
    y = (xf - mu) * lax.rsqrt(var + EPS)
    return (y * g.astype(jnp.float32) + b.astype(jnp.float32)).astype(x.dtype)


def t5_buckets(rel):
    nb = REL_BUCKETS // 2
    max_exact = nb // 2
    ret = jnp.where(rel > 0, nb, 0).astype(jnp.int32)
    n = jnp.abs(rel).astype(jnp.int32)
    nf = jnp.maximum(n, 1).astype(jnp.float32)
    large = max_exact + (jnp.log(nf / max_exact) / math.log(REL_MAX_DIST / max_exact)
                         * (nb - max_exact)).astype(jnp.int32)
    large = jnp.minimum(large, nb - 1)
    return ret + jnp.where(n < max_exact, n, large)


def gla_chunked(q, k, v, g):
    out_dtype = v.dtype
    b_, s_, h_, dk = q.shape
    dv = v.shape[-1]
    n_ch = s_ // GLA_CHUNK

    def to_chunks(t):
        return t.astype(jnp.float32).reshape(b_, n_ch, GLA_CHUNK, h_, t.shape[-1]).transpose(1, 0, 3, 2, 4)

    q, k, v, g = (to_chunks(t) for t in (q, k, v, g))
    cum = jnp.cumsum(g, axis=-2)
    ref = cum[..., GLA_CHUNK // 2 - 1:GLA_CHUNK // 2, :]
    last = cum[..., -1:, :]
    scores = jnp.einsum('nbhid,nbhjd->nbhij', q * jnp.exp(cum - ref), k * jnp.exp(ref - cum))
    lower_tri = jnp.tril(jnp.ones((GLA_CHUNK, GLA_CHUNK), dtype=bool))
    scores = jnp.where(lower_tri, scores, 0.0)
    o_intra = jnp.einsum('nbhij,nbhje->nbhie', scores, v)
    q_inter = q * jnp.exp(cum)
    k_state = k * jnp.exp(last - cum)
    chunk_decay = jnp.exp(last[..., 0, :])

    def step(state, xs):
        qn, kn, vn, dn = xs
        o = jnp.einsum('bhid,bhde->bhie', qn, state)
        state = dn[..., None] * state + jnp.einsum('bhid,bhie->bhde', kn, vn)
        return state, o

    state0 = jnp.zeros((b_, h_, dk, dv), jnp.float32)
    _, o_inter = lax.scan(step, state0, (q_inter, k_state, v, chunk_decay))
    o = (o_intra + o_inter).transpose(1, 0, 3, 2, 4).reshape(b_, s_, h_, dv)
    return o.astype(out_dtype)


def diff_attention(q, k, v, rel_bias, lam):
    b_, s_, h_, _, d = q.shape
    nb = s_ // ATTN_BLOCK
    qb = q.reshape(b_, nb, ATTN_BLOCK, h_, 2, d).transpose(1, 0, 2, 3, 4, 5)
    starts = jnp.arange(nb, dtype=jnp.int32) * ATTN_BLOCK
    k_pos = jnp.arange(s_, dtype=jnp.int32)
    scale = d ** -0.5

    def block(args):
        qblk, start = args
        q_pos = start + jnp.arange(ATTN_BLOCK, dtype=jnp.int32)
        bucket = t5_buckets(k_pos[None, :] - q_pos[:, None])
        bias = jnp.transpose(rel_bias[bucket], (2, 0, 1)).astype(jnp.float32)
        s = jnp.einsum('bqhcd,bkhcd->bhcqk', qblk, k).astype(jnp.float32) * scale + bias[None, :, None]
        p = jax.nn.softmax(s, axis=-1)
        a = p[:, :, 0] - lam * p[:, :, 1]
        return jnp.einsum('bhqk,bkhe->bqhe', a.astype(v.dtype), v)

    out = lax.map(block, (qb, starts))
    return out.transpose(1, 0, 2, 3, 4).reshape(b_, s_, h_, v.shape[-1])


def setup_inputs(seed: int = 0) -> dict:
    key = jax.random.key(seed)
    ks = jax.random.split(key, 18)
    f32 = jnp.float32
    L = DEPTH
    nrm = lambda k, shp: jax.random.normal(k, shp, f32)
    return {
        "x": nrm(ks[0], (BATCH, SEQ, D_MODEL)),
        "norm_pre": 1.0 + 0.02 * nrm(ks[1], (L, D_MODEL)),
        "w_in": nrm(ks[2], (L, D_MODEL, D_IN)) * D_MODEL ** -0.5,
        "gmlp_ln_g": 1.0 + 0.02 * nrm(ks[3], (L, BRANCH_WIDTH)),
        "gmlp_ln_b": 0.02 * nrm(ks[4], (L, BRANCH_WIDTH)),
        "gmlp_ws": nrm(ks[5], (L, GMLP_GROUPS, GMLP_CHUNK, GMLP_CHUNK)) * GMLP_CHUNK ** -0.5,
        "gmlp_bs": 1.0 + 0.1 * nrm(ks[6], (L, GMLP_GROUPS, GMLP_CHUNK)),
        "gla_wa2": nrm(ks[7], (L, 2, GLA_RANK, GLA_HEADS * GLA_DK)) * GLA_RANK ** -0.5,
        "gla_ba": 0.1 * nrm(ks[8], (L, 2, GLA_HEADS * GLA_DK)),
        "gla_norm": 1.0 + 0.02 * nrm(ks[9], (L, GLA_DV)),
        "diff_lambda": 0.1 * nrm(ks[10], (L, 4, DIFF_HEAD_DIM)),
        "diff_norm": 1.0 + 0.02 * nrm(ks[11], (L, DIFF_V_DIM)),
        "rel_bias": 0.3 * nrm(ks[12], (REL_BUCKETS, DIFF_HEADS)),
        "w_branch": nrm(ks[13], (L, N_BRANCH, BRANCH_WIDTH, D_MODEL)) * BRANCH_WIDTH ** -0.5,
        "w_merge": nrm(ks[14], (L, D_MODEL, N_BRANCH * D_MODEL)) * D_MODEL ** -0.5,
        "b_merge": 0.1 * nrm(ks[15], (L, N_BRANCH * D_MODEL)),
        "w_out": nrm(ks[16], (L, D_MODEL, D_MODEL)) * D_MODEL ** -0.5,
        "norm_post": 1.0 + 0.02 * nrm(ks[17], (L, D_MODEL)),
    }


def reference(x, norm_pre, w_in, gmlp_ln_g, gmlp_ln_b, gmlp_ws, gmlp_bs, gla_wa2, gla_ba, gla_norm,
              diff_lambda, diff_norm, rel_bias, w_branch, w_merge, b_merge, w_out, norm_post):
    B, S, _ = x.shape
    split_points = []
    acc = 0
    for size in IN_SIZES[:-1]:
        acc += size
        split_points.append(acc)
    rev = lambda t: jnp.flip(t, axis=1)

    for l in range(DEPTH):
        h = rms_norm(x, norm_pre[l])
        proj = jnp.einsum('bsd,de->bse', h, w_in[l])
        (a_u, a_v, a_z, b_q, b_k, b_v, b_z, b_lr,
         c_q, c_k, c_v, c_z) = jnp.split(proj, split_points, axis=-1)

        u = jax.nn.gelu(a_u)
        sv = layer_norm(jax.nn.gelu(a_v), gmlp_ln_g[l], gmlp_ln_b[l])
        sv = sv.reshape(B, S // GMLP_CHUNK, GMLP_CHUNK, GMLP_GROUPS, GMLP_GROUP_CH)
        sv = jnp.einsum('gpq,bnqgc->bnpgc', gmlp_ws[l], sv) + gmlp_bs[l].T[None, None, :, :, None]
        y_a = u * sv.reshape(B, S, BRANCH_WIDTH)

        q = b_q.reshape(B, S, GLA_HEADS, GLA_DK) * GLA_DK ** -0.5
        k = b_k.reshape(B, S, GLA_HEADS, GLA_DK)
        v = b_v.reshape(B, S, GLA_HEADS, GLA_DV)
        lr_f = b_lr[..., :GLA_RANK]
        lr_b = b_lr[..., GLA_RANK:]
        g_f = jax.nn.log_sigmoid((jnp.einsum('bsr,rk->bsk', lr_f, gla_wa2[l, 0]) + gla_ba[l, 0])
                                 .astype(jnp.float32)) / GLA_TAU
        g_b = jax.nn.log_sigmoid((jnp.einsum('bsr,rk->bsk', lr_b, gla_wa2[l, 1]) + gla_ba[l, 1])
                                 .astype(jnp.float32)) / GLA_TAU
        g_f = g_f.reshape(B, S, GLA_HEADS, GLA_DK)
        g_b = g_b.reshape(B, S, GLA_HEADS, GLA_DK)
        o_f = gla_chunked(q, k, v, g_f)
        o_b = rev(gla_chunked(rev(q), rev(k), rev(v), rev(g_b)))
        y_b = rms_norm(o_f + o_b, gla_norm[l]).reshape(B, S, GLA_HEADS * GLA_DV)

        lam_init = 0.8 - 0.6 * math.exp(-0.3 * l)
        lv = diff_lambda[l].astype(jnp.float32)
        lam = jnp.exp(jnp.sum(lv[0] * lv[1])) - jnp.exp(jnp.sum(lv[2] * lv[3])) + lam_init
        qc = c_q.reshape(B, S, DIFF_HEADS, 2, DIFF_HEAD_DIM)
        kc = c_k.reshape(B, S, DIFF_HEADS, 2, DIFF_HEAD_DIM)
        vc = c_v.reshape(B, S, DIFF_HEADS, DIFF_V_DIM)
        o_c = diff_attention(qc, kc, vc, rel_bias, lam)
        y_c = (rms_norm(o_c, diff_norm[l]) * (1.0 - lam_init)).reshape(B, S, DIFF_HEADS * DIFF_V_DIM)

        branches = jnp.stack([y_a * jax.nn.silu(a_z), y_b * jax.nn.silu(b_z), y_c * jax.nn.silu(c_z)], axis=2)
        proj_b = jnp.einsum('bsiw,iwd->bsid', branches, w_branch[l])
        gates = jax.nn.sigmoid(jnp.einsum('bsd,de->bse', h, w_merge[l]) + b_merge[l])
        gates = gates.reshape(B, S, N_BRANCH, D_MODEL)
        merged = jnp.sum(gates * proj_b, axis=2)
        out = jnp.einsum('bsd,de->bse', merged, w_out[l])
        x = x + rms_norm(out, norm_post[l])
    return x
```

```python
import functools
import math

import numpy as np
import jax
import jax.numpy as jnp
from jax import lax
from jax.experimental import pallas as pl
from jax.experimental.pallas import tpu as pltpu

F32 = jnp.float32
BF16 = jnp.bfloat16

EPS = 1e-6
N_BRANCH = 3
GMLP_CHUNK = 128
GMLP_GROUPS = 8
GLA_HEADS = 4
GLA_RANK = 16
GLA_TAU = 16.0
GLA_SUB = 64
GLA_CHUNK = 2 * GLA_SUB
DIFF_HEADS = 8
ATTN_BLOCK = 128
REL_BUCKETS = 32
T5_STARTS = (1, 2, 3, 4, 5, 6, 7, 8, 12, 16, 23, 32, 46, 64, 91)

LANES = 128
VMEM_LIMIT = 52 * 1024 * 1024


def _params(*sem):
    return pltpu.CompilerParams(dimension_semantics=sem, vmem_limit_bytes=VMEM_LIMIT)


def _gelu(x):
    return 0.5 * x * (1.0 + jnp.tanh(math.sqrt(2.0 / math.pi) * (x + 0.044715 * (x * x * x))))


def _sigmoid(x):
    return 1.0 / (1.0 + jnp.exp(-x))


def _silu(x):
    return x * _sigmoid(x)


def _log_sigmoid(x):
    return jnp.minimum(x, 0.0) - jnp.log1p(jnp.exp(-jnp.abs(x)))


def _dot(a, b):
    return jnp.dot(a, b, preferred_element_type=F32)


def _dot_nt(a, b):
    return lax.dot_general(a, b, (((1,), (1,)), ((), ())), preferred_element_type=F32)


def _rms_kernel(x_ref, g_ref, o_ref):
    x = x_ref[...]
    y = x * lax.rsqrt(jnp.mean(x * x, axis=-1, keepdims=True) + EPS)
    o_ref[...] = (y * g_ref[...]).astype(o_ref.dtype)


def _rms_call(x2, g):
    m, d = x2.shape
    tm = min(512, m)
    return pl.pallas_call(
        _rms_kernel,
        grid=(m // tm,),
        in_specs=[pl.BlockSpec((tm, d), lambda i: (i, 0)),
                  pl.BlockSpec((1, d), lambda i: (0, 0))],
        out_specs=pl.BlockSpec((tm, d), lambda i: (i, 0)),
        out_shape=jax.ShapeDtypeStruct((m, d), BF16),
        compiler_params=_params("parallel"),
        name="rms_pre",
    )(x2, g.reshape(1, d))


def _mm_kernel(a_ref, w_ref, o_ref):
    o_ref[...] = _dot(a_ref[...], w_ref[...]).astype(o_ref.dtype)


def _mm_call(a, w, name):
    m, k = a.shape
    n = w.shape[1]
    tm = min(1024, m)
    tn = min(1024, n)
    return pl.pallas_call(
        _mm_kernel,
        grid=(m // tm, n // tn),
        in_specs=[pl.BlockSpec((tm, k), lambda i, j: (i, 0)),
                  pl.BlockSpec((k, tn), lambda i, j: (0, j))],
        out_specs=pl.BlockSpec((tm, tn), lambda i, j: (i, j)),
        out_shape=jax.ShapeDtypeStruct((m, n), BF16),
        compiler_params=_params("parallel", "arbitrary"),
        name=name,
    )(a, w)


def _gmlp_kernel(u_ref, v_ref, z_ref, lng_ref, lnb_ref, ws_ref, bs_ref, o_ref):
    tm, w = v_ref.shape
    gc = w // GMLP_GROUPS
    gv = _gelu(v_ref[...].astype(F32))
    mu = jnp.mean(gv, axis=-1, keepdims=True)
    cen = gv - mu
    var = jnp.mean(cen * cen, axis=-1, keepdims=True)
    sv = (cen * lax.rsqrt(var + EPS) * lng_ref[...] + lnb_ref[...]).astype(BF16)
    for c in range(tm // GMLP_CHUNK):
        rows = slice(c * GMLP_CHUNK, (c + 1) * GMLP_CHUNK)
        for g in range(GMLP_GROUPS):
            cols = slice(g * gc, (g + 1) * gc)
            mixed = _dot(ws_ref[g], sv[rows, cols]) + bs_ref[:, g:g + 1]
            u = _gelu(u_ref[rows, cols].astype(F32))
            z = _silu(z_ref[rows, cols].astype(F32))
            o_ref[rows, cols] = (u * mixed * z).astype(o_ref.dtype)


def _gmlp_call(proj, lng, lnb, ws, bs_t, width):
    m = proj.shape[0]
    tm = min(256, m)
    blk = lambda c: pl.BlockSpec((tm, width), lambda i: (i, c))
    full = lambda a: pl.BlockSpec(a.shape, lambda i: (0,) * a.ndim)
    return pl.pallas_call(
        _gmlp_kernel,
        grid=(m // tm,),
        in_specs=[blk(0), blk(1), blk(2), full(lng), full(lnb), full(ws), full(bs_t)],
        out_specs=pl.BlockSpec((tm, width), lambda i: (i, 0)),
        out_shape=jax.ShapeDtypeStruct((m, width), BF16),
        compiler_params=_params("parallel"),
        name="gmlp",
    )(proj, proj, proj, lng, lnb, ws, bs_t)


def _gla_constants():
    c, s = GLA_CHUNK, GLA_SUB
    i = np.arange(c)[:, None]
    j = np.arange(c)[None, :]
    same = (i // s) == (j // s)
    mats = []
    for rev in (False, True):
        if not rev:
            cum = same & (j <= i)
            ref = same & ((j % s) <= s // 2 - 1)
        else:
            cum = same & (j >= i)
            ref = same & ((j % s) >= s // 2)
        cum, ref, tot = (t.astype(np.float32) for t in (cum, ref, same))
        mats.append(np.concatenate([cum, cum - ref, tot - cum], axis=0))
    return np.stack(mats)


def _gla_kernel(q_ref, k_ref, v_ref, z_ref, lr_ref, wa_ref, ba_ref, gn_ref, cm_ref, o_ref,
                vt_scr, st_scr, o_scr):
    s_len, dk = q_ref.shape
    dv = v_ref.shape[1]
    c, sub = GLA_CHUNK, GLA_SUB
    n_chunks = s_len // c

    vt_scr[...] = v_ref[...].astype(F32).T.astype(BF16)

    ri = lax.broadcasted_iota(jnp.int32, (c, c), 0)
    ci = lax.broadcasted_iota(jnp.int32, (c, c), 1)
    same = (ri // sub) == (ci // sub)
    row_in_sub0 = lax.broadcasted_iota(jnp.int32, (c, 1), 0) < sub

    def run_direction(rev):
        d = 1 if rev else 0
        cmat = cm_ref[d]
        wa = wa_ref[d]
        ba = ba_ref[d]
        if rev:
            diag_mask = same & (ci >= ri)
            off_mask = (ri < sub) & (ci >= sub)
        else:
            diag_mask = same & (ci <= ri)
            off_mask = (ri >= sub) & (ci < sub)
        st_scr[...] = jnp.zeros_like(st_scr)

        def body(t, carry):
            n = (n_chunks - 1 - t) if rev else t
            r0 = pl.multiple_of(n * c, c)
            rows = pl.ds(r0, c)
            q = q_ref[rows, :].astype(F32) * (dk ** -0.5)
            k = k_ref[rows, :].astype(F32)
            v = v_ref[rows, :]
            g = _log_sigmoid(_dot(lr_ref[rows, :], wa) + ba) * (1.0 / GLA_TAU)
            g_hi = g.astype(BF16)
            g_lo = (g - g_hi.astype(F32)).astype(BF16)
            e = _dot(cmat, g_hi) + _dot(cmat, g_lo)
            cum, cum_ref, rest = e[:c], e[c:2 * c], e[2 * c:]
            if rev:
                last0, last1 = cum[0:1], cum[sub:sub + 1]
            else:
                last0, last1 = cum[sub - 1:sub], cum[c - 1:c]
            q_in = (q * jnp.exp(cum_ref)).astype(BF16)
            k_in = (k * jnp.exp(-cum_ref)).astype(BF16)
            q_dec = q * jnp.exp(cum)
            k_dec = k * jnp.exp(rest)
            s_diag = _dot_nt(q_in, k_in)
            s_off = _dot_nt(q_dec.astype(BF16), k_dec.astype(BF16))
            scores = jnp.where(diag_mask, s_diag, 0.0) + jnp.where(off_mask, s_off, 0.0)
            if rev:
                q_st = q_dec * jnp.where(row_in_sub0, jnp.exp(last1), 1.0)
                k_st = k_dec * jnp.where(row_in_sub0, 1.0, jnp.exp(last0))
            else:
                q_st = q_dec * jnp.where(row_in_sub0, 1.0, jnp.exp(last0))
                k_st = k_dec * jnp.where(row_in_sub0, jnp.exp(last1), 1.0)
            st = st_scr[...]
            o = _dot(scores.astype(BF16), v) + _dot_nt(q_st.astype(BF16), st.astype(BF16))
            if rev:
                o_scr[rows, :] += o
            else:
                o_scr[rows, :] = o
            kv_t = _dot(vt_scr[:, rows], k_st.astype(BF16))
            st_scr[...] = st * jnp.exp(last0 + last1) + kv_t
            return carry

        lax.fori_loop(0, n_chunks, body, 0)

    run_direction(False)
    run_direction(True)

    o = o_scr[...]
    y = o * lax.rsqrt(jnp.mean(o * o, axis=-1, keepdims=True) + EPS) * gn_ref[...]
    o_ref[...] = (y * _silu(z_ref[...].astype(F32))).astype(o_ref.dtype)


def _gla_call(proj, lr, wa, ba, gnorm, cmats, batch, seq, width, col0):
    m = proj.shape[0]
    h = GLA_HEADS
    dv = width // h
    dk = dv // 2
    qb, kb = col0 // dk, col0 // dk + h
    vb, zb = (col0 + 2 * h * dk) // dv, (col0 + 2 * h * dk) // dv + h
    return pl.pallas_call(
        _gla_kernel,
        grid=(batch, h),
        in_specs=[pl.BlockSpec((seq, dk), lambda b, i: (b, qb + i)),
                  pl.BlockSpec((seq, dk), lambda b, i: (b, kb + i)),
                  pl.BlockSpec((seq, dv), lambda b, i: (b, vb + i)),
                  pl.BlockSpec((seq, dv), lambda b, i: (b, zb + i)),
                  pl.BlockSpec((seq, LANES), lambda b, i: (b, 0)),
                  pl.BlockSpec((2, LANES, dk), lambda b, i: (0, 0, i)),
                  pl.BlockSpec((2, 1, dk), lambda b, i: (0, 0, i)),
                  pl.BlockSpec((1, dv), lambda b, i: (0, 0)),
                  pl.BlockSpec(cmats.shape, lambda b, i: (0, 0, 0))],
        out_specs=pl.BlockSpec((seq, dv), lambda b, i: (b, i)),
        out_shape=jax.ShapeDtypeStruct((m, width), BF16),
        scratch_shapes=[pltpu.VMEM((dv, seq), BF16),
                        pltpu.VMEM((dv, dk), F32),
                        pltpu.VMEM((seq, dv), F32)],
        compiler_params=_params("parallel", "parallel"),
        name="gla",
    )(proj, proj, proj, proj, lr, wa, ba, gnorm, cmats)


def _attn_kernel(lam_init, relb_ref, lam_ref, dn_ref, q_ref, k_ref, v_ref, z_ref, o_ref, bias_scr):
    h = pl.program_id(0)
    tq, d2 = q_ref.shape
    s_len = k_ref.shape[0]
    d = d2 // 2
    blk = ATTN_BLOCK
    n_tiles = bias_scr.shape[1] // LANES

    @pl.when((pl.program_id(1) == 0) & (pl.program_id(2) == 0))
    def _():
        def tile(t, carry):
            c0 = pl.multiple_of(t * LANES, LANES)
            rel = (lax.broadcasted_iota(jnp.int32, (blk, LANES), 1) + (c0 - (s_len - blk))
                   - lax.broadcasted_iota(jnp.int32, (blk, LANES), 0))
            dist = jnp.abs(rel)
            neg = jnp.full((blk, LANES), relb_ref[0, h], F32)
            pos = jnp.full((blk, LANES), relb_ref[REL_BUCKETS // 2, h], F32)
            for bucket, start in enumerate(T5_STARTS, start=1):
                far = dist >= start
                neg = jnp.where(far, relb_ref[bucket, h], neg)
                pos = jnp.where(far, relb_ref[REL_BUCKETS // 2 + bucket, h], pos)
            bias_scr[:, pl.ds(c0, LANES)] = jnp.where(rel > 0, pos, neg)
            return carry
        lax.fori_loop(0, n_tiles, tile, 0)

    lv = lam_ref[...]
    lam = (jnp.exp(jnp.sum(lv[0:1] * lv[1:2], axis=-1, keepdims=True))
           - jnp.exp(jnp.sum(lv[2:3] * lv[3:4], axis=-1, keepdims=True)) + lam_init)

    k1 = k_ref[:, :d]
    k2 = k_ref[:, d:]
    v = v_ref[...]
    scale = d ** -0.5
    for r in range(tq // blk):
        rows = slice(r * blk, (r + 1) * blk)
        q_start = pl.program_id(2) * tq + r * blk
        off = pl.multiple_of(s_len - blk - q_start, blk)
        bias = bias_scr[:, pl.ds(off, s_len)]
        q = q_ref[rows, :]
        s1 = _dot_nt(q[:, :d], k1) * scale + bias
        s2 = _dot_nt(q[:, d:], k2) * scale + bias
        p1 = jnp.exp(s1 - jnp.max(s1, axis=-1, keepdims=True))
        p2 = jnp.exp(s2 - jnp.max(s2, axis=-1, keepdims=True))
        w1 = 1.0 / jnp.sum(p1, axis=-1, keepdims=True)
        w2 = lam / jnp.sum(p2, axis=-1, keepdims=True)
        a = (p1 * w1 - p2 * w2).astype(BF16)
        o = _dot(a, v)
        y = o * lax.rsqrt(jnp.mean(o * o, axis=-1, keepdims=True) + EPS) * dn_ref[...]
        y = y * (1.0 - lam_init) * _silu(z_ref[rows, :].astype(F32))
        o_ref[rows, :] = y.astype(o_ref.dtype)


def _attn_call(proj, rel_bias, lam_par, dnorm, lam_init, batch, seq, width, col0):
    m = proj.shape[0]
    nh = DIFF_HEADS
    hw = width // nh
    tq = min(512, seq)
    nq = seq // tq
    c0 = col0 // hw
    return pl.pallas_call(
        functools.partial(_attn_kernel, lam_init),
        grid=(nh, batch, nq),
        in_specs=[pl.BlockSpec(memory_space=pltpu.SMEM),
                  pl.BlockSpec(lam_par.shape, lambda h, b, i: (0, 0)),
                  pl.BlockSpec((1, hw), lambda h, b, i: (0, 0)),
                  pl.BlockSpec((tq, hw), lambda h, b, i: (b * nq + i, c0 + h)),
                  pl.BlockSpec((seq, hw), lambda h, b, i: (b, c0 + nh + h)),
                  pl.BlockSpec((seq, hw), lambda h, b, i: (b, c0 + 2 * nh + h)),
                  pl.BlockSpec((tq, hw), lambda h, b, i: (b * nq + i, c0 + 3 * nh + h))],
        out_specs=pl.BlockSpec((tq, hw), lambda h, b, i: (b * nq + i, h)),
        out_shape=jax.ShapeDtypeStruct((m, width), BF16),
        scratch_shapes=[pltpu.VMEM((ATTN_BLOCK, 2 * seq - ATTN_BLOCK), F32)],
        compiler_params=_params("arbitrary", "arbitrary", "arbitrary"),
        name="diff_attn",
    )(rel_bias, lam_par, dnorm, proj, proj, proj, proj)


def _merge_kernel(h_ref, za_ref, zb_ref, zc_ref, wm0_ref, wm1_ref, wm2_ref,
                  bm0_ref, bm1_ref, bm2_ref, wb0_ref, wb1_ref, wb2_ref, o_ref):
    hh = h_ref[...]
    acc = None
    for z_ref, wm_ref, bm_ref, wb_ref in ((za_ref, wm0_ref, bm0_ref, wb0_ref),
                                          (zb_ref, wm1_ref, bm1_ref, wb1_ref),
                                          (zc_ref, wm2_ref, bm2_ref, wb2_ref)):
        gate = _sigmoid(_dot(hh, wm_ref[...]) + bm_ref[...])
        term = gate * _dot(z_ref[...], wb_ref[0])
        acc = term if acc is None else acc + term
    o_ref[...] = acc.astype(o_ref.dtype)


def _merge_call(h, za, zb, zc, wm, bm, wb):
    m, d = h.shape
    tm = min(512, m)
    tn = min(256, d)
    nj = d // tn
    row = pl.BlockSpec((tm, d), lambda i, j: (i, 0))
    wm_spec = lambda br: pl.BlockSpec((d, tn), lambda i, j: (0, br * nj + j))
    bm_spec = lambda br: pl.BlockSpec((1, tn), lambda i, j: (0, br * nj + j))
    wb_spec = lambda br: pl.BlockSpec((1, d, tn), lambda i, j: (br, 0, j))
    return pl.pallas_call(
        _merge_kernel,
        grid=(m // tm, nj),
        in_specs=[row, row, row, row,
                  wm_spec(0), wm_spec(1), wm_spec(2),
                  bm_spec(0), bm_spec(1), bm_spec(2),
                  wb_spec(0), wb_spec(1), wb_spec(2)],
        out_specs=pl.BlockSpec((tm, tn), lambda i, j: (i, j)),
        out_shape=jax.ShapeDtypeStruct((m, d), BF16),
        compiler_params=_params("parallel", "arbitrary"),
        name="merge",
    )(h, za, zb, zc, wm, wm, wm, bm, bm, bm, wb, wb, wb)


def _out_kernel(mg_ref, w_ref, x_ref, g_ref, o_ref):
    out = _dot(mg_ref[...], w_ref[...])
    y = out * lax.rsqrt(jnp.mean(out * out, axis=-1, keepdims=True) + EPS) * g_ref[...]
    o_ref[...] = x_ref[...] + y


def _out_call(merged, w, x2, g):
    m, d = x2.shape
    tm = min(512, m)
    return pl.pallas_call(
        _out_kernel,
        grid=(m // tm,),
        in_specs=[pl.BlockSpec((tm, d), lambda i: (i, 0)),
                  pl.BlockSpec((d, d), lambda i: (0, 0)),
                  pl.BlockSpec((tm, d), lambda i: (i, 0)),
                  pl.BlockSpec((1, d), lambda i: (0, 0))],
        out_specs=pl.BlockSpec((tm, d), lambda i: (i, 0)),
        out_shape=jax.ShapeDtypeStruct((m, d), F32),
        compiler_params=_params("parallel"),
        name="out_proj",
    )(merged, w, x2, g.reshape(1, d))


def kernel(x, norm_pre, w_in, gmlp_ln_g, gmlp_ln_b, gmlp_ws, gmlp_bs, gla_wa2, gla_ba, gla_norm,
           diff_lambda, diff_norm, rel_bias, w_branch, w_merge, b_merge, w_out, norm_post):
    batch, seq, d = x.shape
    depth = norm_pre.shape[0]
    width = d
    m = batch * seq
    hk = gla_wa2.shape[-1]
    lr_col = 3 * width + 2 * hk + 2 * width
    lr_w = 2 * GLA_RANK
    assert seq % GLA_CHUNK == 0 and seq % ATTN_BLOCK == 0 and seq % GMLP_CHUNK == 0
    assert w_in.shape[-1] == lr_col + lr_w + 4 * width

    cmats = jnp.asarray(_gla_constants(), BF16)
    x2 = x.reshape(m, d)
    for l in range(depth):
        w_main = jnp.concatenate([w_in[l, :, :lr_col], w_in[l, :, lr_col + lr_w:]], axis=1).astype(BF16)
        w_lr = jnp.pad(w_in[l, :, lr_col:lr_col + lr_w], ((0, 0), (0, LANES - lr_w))).astype(BF16)
        wa = jnp.zeros((2, LANES, hk), F32)
        wa = wa.at[0, :GLA_RANK].set(gla_wa2[l, 0]).at[1, GLA_RANK:lr_w].set(gla_wa2[l, 1]).astype(BF16)
        ba = gla_ba[l].reshape(2, 1, hk)

        h = _rms_call(x2, norm_pre[l])
        proj = _mm_call(h, w_main, "in_proj")
        lr = _mm_call(h, w_lr, "lr_proj")

        za = _gmlp_call(proj, gmlp_ln_g[l].reshape(1, width), gmlp_ln_b[l].reshape(1, width),
                        gmlp_ws[l].astype(BF16), gmlp_bs[l].T, width)
        zb = _gla_call(proj, lr, wa, ba, gla_norm[l].reshape(1, -1), cmats, batch, seq, width,
                       3 * width)
        lam_init = 0.8 - 0.6 * math.exp(-0.3 * l)
        zc = _attn_call(proj, rel_bias, diff_lambda[l], diff_norm[l].reshape(1, -1), lam_init,
                        batch, seq, width, lr_col)
        merged = _merge_call(h, za, zb, zc, w_merge[l].astype(BF16), b_merge[l].reshape(1, -1),
                             w_branch[l].astype(BF16))
        x2 = _out_call(merged, w_out[l].astype(BF16), x2, norm_post[l])
    return x2.reshape(batch, seq, d)
```

```python
import functools
import math

import numpy as np
import jax
import jax.numpy as jnp
from jax import lax
from jax.experimental import pallas as pl
from jax.experimental.pallas import tpu as pltpu

F32 = jnp.float32
BF16 = jnp.bfloat16

EPS = 1e-6
LOG2E = math.log2(math.e)
N_BRANCH = 3
GMLP_CHUNK = 128
GMLP_GROUPS = 8
GLA_HEADS = 4
GLA_RANK = 16
GLA_TAU = 16.0
GLA_SUB = 64
GLA_CHUNK = 2 * GLA_SUB
DIFF_HEADS = 8
ATTN_Q_ROWS = 256
ATTN_KEY_TILE = 256
REL_BUCKETS = 32
T5_STARTS = (1, 2, 3, 4, 5, 6, 7, 8, 12, 16, 23, 32, 46, 64, 91)

LANES = 128
VMEM_LIMIT = 52 * 1024 * 1024


def _params(*sem):
    return pltpu.CompilerParams(dimension_semantics=sem, vmem_limit_bytes=VMEM_LIMIT)


def _gelu(x):
    return 0.5 * x * (1.0 + jnp.tanh(math.sqrt(2.0 / math.pi) * (x + 0.044715 * (x * x * x))))


def _sigmoid(x):
    return 1.0 / (1.0 + jnp.exp(-x))


def _silu(x):
    return x * _sigmoid(x)


def _log_sigmoid(x):
    return jnp.minimum(x, 0.0) - jnp.log1p(jnp.exp(-jnp.abs(x)))


def _dot(a, b):
    return jnp.dot(a, b, preferred_element_type=F32)


def _dot_nt(a, b):
    return lax.dot_general(a, b, (((1,), (1,)), ((), ())), preferred_element_type=F32)


def _rms_kernel(x_ref, g_ref, o_ref):
    x = x_ref[...]
    y = x * lax.rsqrt(jnp.mean(x * x, axis=-1, keepdims=True) + EPS)
    o_ref[...] = (y * g_ref[...]).astype(o_ref.dtype)


def _rms_call(x2, g):
    m, d = x2.shape
    tm = min(512, m)
    return pl.pallas_call(
        _rms_kernel,
        grid=(m // tm,),
        in_specs=[pl.BlockSpec((tm, d), lambda i: (i, 0)),
                  pl.BlockSpec((1, d), lambda i: (0, 0))],
        out_specs=pl.BlockSpec((tm, d), lambda i: (i, 0)),
        out_shape=jax.ShapeDtypeStruct((m, d), BF16),
        compiler_params=_params("parallel"),
        name="rms_pre",
    )(x2, g.reshape(1, d))


def _mm_kernel(a_ref, w_ref, o_ref):
    o_ref[...] = _dot(a_ref[...], w_ref[...]).astype(o_ref.dtype)


def _mm_call(a, w, name):
    m, k = a.shape
    n = w.shape[1]
    tm = min(1024, m)
    tn = min(1024, n)
    return pl.pallas_call(
        _mm_kernel,
        grid=(m // tm, n // tn),
        in_specs=[pl.BlockSpec((tm, k), lambda i, j: (i, 0)),
                  pl.BlockSpec((k, tn), lambda i, j: (0, j))],
        out_specs=pl.BlockSpec((tm, tn), lambda i, j: (i, j)),
        out_shape=jax.ShapeDtypeStruct((m, n), BF16),
        compiler_params=_params("parallel", "arbitrary"),
        name=name,
    )(a, w)


def _gmlp_kernel(u_ref, v_ref, z_ref, lng_ref, lnb_ref, ws_ref, bs_ref, o_ref):
    tm, w = v_ref.shape
    gc = w // GMLP_GROUPS
    gv = _gelu(v_ref[...].astype(F32))
    mu = jnp.mean(gv, axis=-1, keepdims=True)
    cen = gv - mu
    var = jnp.mean(cen * cen, axis=-1, keepdims=True)
    sv = (cen * lax.rsqrt(var + EPS) * lng_ref[...] + lnb_ref[...]).astype(BF16)
    for c in range(tm // GMLP_CHUNK):
        rows = slice(c * GMLP_CHUNK, (c + 1) * GMLP_CHUNK)
        for g in range(GMLP_GROUPS):
            cols = slice(g * gc, (g + 1) * gc)
            mixed = _dot(ws_ref[g], sv[rows, cols]) + bs_ref[:, g:g + 1]
            u = _gelu(u_ref[rows, cols].astype(F32))
            z = _silu(z_ref[rows, cols].astype(F32))
            o_ref[rows, cols] = (u * mixed * z).astype(o_ref.dtype)


def _gmlp_call(proj, lng, lnb, ws, bs_t, width):
    m = proj.shape[0]
    tm = min(256, m)
    blk = lambda c: pl.BlockSpec((tm, width), lambda i: (i, c))
    full = lambda a: pl.BlockSpec(a.shape, lambda i: (0,) * a.ndim)
    return pl.pallas_call(
        _gmlp_kernel,
        grid=(m // tm,),
        in_specs=[blk(0), blk(1), blk(2), full(lng), full(lnb), full(ws), full(bs_t)],
        out_specs=pl.BlockSpec((tm, width), lambda i: (i, 0)),
        out_shape=jax.ShapeDtypeStruct((m, width), BF16),
        compiler_params=_params("parallel"),
        name="gmlp",
    )(proj, proj, proj, lng, lnb, ws, bs_t)


def _gla_constants():
    c, s = GLA_CHUNK, GLA_SUB
    i = np.arange(c)[:, None]
    j = np.arange(c)[None, :]
    same = (i // s) == (j // s)
    mats = []
    for rev in (False, True):
        if not rev:
            cum = same & (j <= i)
            ref = same & ((j % s) <= s // 2 - 1)
        else:
            cum = same & (j >= i)
            ref = same & ((j % s) >= s // 2)
        cum, ref, tot = (t.astype(np.float32) for t in (cum, ref, same))
        blocks = np.concatenate([cum, cum - ref, ref - cum, tot - cum], axis=0)
        mats.append(np.concatenate([blocks, blocks], axis=1))
    return np.stack(mats)


def _gla_kernel(q_ref, k_ref, v_ref, z_ref, lr_ref, wa_ref, ba_ref, gn_ref, cm_ref, o_ref,
                vt_scr, st_scr, o_scr):
    s_len, dk = q_ref.shape
    c, sub = GLA_CHUNK, GLA_SUB
    n_chunks = s_len // c

    vt_scr[...] = v_ref[...].astype(F32).T.astype(BF16)
    st_scr[...] = jnp.zeros_like(st_scr)
    o_scr[...] = jnp.zeros_like(o_scr)

    ri = lax.broadcasted_iota(jnp.int32, (c, c), 0)
    ci = lax.broadcasted_iota(jnp.int32, (c, c), 1)
    same = (ri // sub) == (ci // sub)
    row_in_sub0 = lax.broadcasted_iota(jnp.int32, (c, 1), 0) < sub

    def chunk(d, n):
        rev = d == 1
        rows = pl.ds(pl.multiple_of(n * c, c), c)
        q = q_ref[rows, :].astype(F32) * (dk ** -0.5)
        k = k_ref[rows, :].astype(F32)
        g = _log_sigmoid(_dot(lr_ref[rows, :], wa_ref[d]) + ba_ref[d]) * (LOG2E / GLA_TAU)
        g_hi = g.astype(BF16)
        g_lo = (g - g_hi.astype(F32)).astype(BF16)
        e = _dot(cm_ref[d], jnp.concatenate([g_hi, g_lo], axis=0))
        cum, to_ref, from_ref, rest = e[:c], e[c:2 * c], e[2 * c:3 * c], e[3 * c:]
        if rev:
            last0, last1 = cum[0:1], cum[sub:sub + 1]
            diag_mask = same & (ci >= ri)
            off_mask = (ri < sub) & (ci >= sub)
        else:
            last0, last1 = cum[sub - 1:sub], cum[c - 1:c]
            diag_mask = same & (ci <= ri)
            off_mask = (ri >= sub) & (ci < sub)
        q_in = (q * jnp.exp2(to_ref)).astype(BF16)
        k_in = (k * jnp.exp2(from_ref)).astype(BF16)
        q_dec = q * jnp.exp2(cum)
        k_dec = k * jnp.exp2(rest)
        s_diag = _dot_nt(q_in, k_in)
        s_off = _dot_nt(q_dec.astype(BF16), k_dec.astype(BF16))
        scores = jnp.where(diag_mask, s_diag, 0.0) + jnp.where(off_mask, s_off, 0.0)
        if rev:
            q_st = q_dec * jnp.where(row_in_sub0, jnp.exp2(last1), 1.0)
            k_st = k_dec * jnp.where(row_in_sub0, 1.0, jnp.exp2(last0))
        else:
            q_st = q_dec * jnp.where(row_in_sub0, 1.0, jnp.exp2(last0))
            k_st = k_dec * jnp.where(row_in_sub0, jnp.exp2(last1), 1.0)
        st = st_scr[d]
        o_scr[rows, :] += (_dot(scores.astype(BF16), v_ref[rows, :])
                           + _dot_nt(q_st.astype(BF16), st.astype(BF16)))
        kv_t = _dot(vt_scr[:, rows], k_st.astype(BF16))
        st_scr[d] = st * jnp.exp2(last0 + last1) + kv_t

    def body(t, carry):
        chunk(0, t)
        chunk(1, n_chunks - 1 - t)
        return carry

    lax.fori_loop(0, n_chunks, body, 0, unroll=2 if n_chunks % 2 == 0 else 1)

    o = o_scr[...]
    y = o * lax.rsqrt(jnp.mean(o * o, axis=-1, keepdims=True) + EPS) * gn_ref[...]
    o_ref[...] = (y * _silu(z_ref[...].astype(F32))).astype(o_ref.dtype)


def _gla_call(proj, lr, wa, ba, gnorm, cmats, batch, seq, width, col0):
    m = proj.shape[0]
    h = GLA_HEADS
    dv = width // h
    dk = dv // 2
    qb, kb = col0 // dk, col0 // dk + h
    vb, zb = (col0 + 2 * h * dk) // dv, (col0 + 2 * h * dk) // dv + h
    return pl.pallas_call(
        _gla_kernel,
        grid=(batch, h),
        in_specs=[pl.BlockSpec((seq, dk), lambda b, i: (b, qb + i)),
                  pl.BlockSpec((seq, dk), lambda b, i: (b, kb + i)),
                  pl.BlockSpec((seq, dv), lambda b, i: (b, vb + i)),
                  pl.BlockSpec((seq, dv), lambda b, i: (b, zb + i)),
                  pl.BlockSpec((seq, LANES), lambda b, i: (b, 0)),
                  pl.BlockSpec((2, LANES, dk), lambda b, i: (0, 0, i)),
                  pl.BlockSpec((2, 1, dk), lambda b, i: (0, 0, i)),
                  pl.BlockSpec((1, dv), lambda b, i: (0, 0)),
                  pl.BlockSpec(cmats.shape, lambda b, i: (0, 0, 0))],
        out_specs=pl.BlockSpec((seq, dv), lambda b, i: (b, i)),
        out_shape=jax.ShapeDtypeStruct((m, width), BF16),
        scratch_shapes=[pltpu.VMEM((dv, seq), BF16),
                        pltpu.VMEM((2, dv, dk), F32),
                        pltpu.VMEM((seq, dv), F32)],
        compiler_params=_params("parallel", "parallel"),
        name="gla",
    )(proj, proj, proj, proj, lr, wa, ba, gnorm, cmats)


def _attn_kernel(lam_init, relb_ref, lam_ref, dn_ref, q_ref, k_ref, v_ref, z_ref, o_ref,
                 bias_scr, s_scr, m_scr):
    h = pl.program_id(0)
    s_len, d2 = q_ref.shape
    d = d2 // 2
    tq = bias_scr.shape[0]
    kt = min(ATTN_KEY_TILE, s_len)
    n_kt = s_len // kt
    n_tiles = bias_scr.shape[1] // LANES

    @pl.when(pl.program_id(1) == 0)
    def _():
        def tile(t, carry):
            c0 = pl.multiple_of(t * LANES, LANES)
            rel = (lax.broadcasted_iota(jnp.int32, (tq, LANES), 1) + (c0 - (s_len - tq))
                   - lax.broadcasted_iota(jnp.int32, (tq, LANES), 0))
            dist = jnp.abs(rel)
            neg = jnp.full((tq, LANES), relb_ref[0, h], F32)
            pos = jnp.full((tq, LANES), relb_ref[REL_BUCKETS // 2, h], F32)
            for bucket, start in enumerate(T5_STARTS, start=1):
                far = dist >= start
                neg = jnp.where(far, relb_ref[bucket, h], neg)
                pos = jnp.where(far, relb_ref[REL_BUCKETS // 2 + bucket, h], pos)
            bias_scr[:, pl.ds(c0, LANES)] = jnp.where(rel > 0, pos, neg) * LOG2E
            return carry
        lax.fori_loop(0, n_tiles, tile, 0)

    lv = lam_ref[...]
    lam = (jnp.exp(jnp.sum(lv[0:1] * lv[1:2], axis=-1, keepdims=True))
           - jnp.exp(jnp.sum(lv[2:3] * lv[3:4], axis=-1, keepdims=True)) + lam_init)

    keys = [slice(j * kt, (j + 1) * kt) for j in range(n_kt)]
    n_blocks = s_len // tq

    def logits_stage(i, slot):
        r0 = pl.multiple_of(i * tq, tq)
        off = s_len - tq - r0
        q = (q_ref[pl.ds(r0, tq), :].astype(F32) * (d ** -0.5 * LOG2E)).astype(BF16)
        for m in range(2):
            mx = jnp.full((tq, LANES), -jnp.inf, F32)
            for j in range(n_kt):
                t = (_dot_nt(q[:, m * d:(m + 1) * d], k_ref[keys[j], m * d:(m + 1) * d])
                     + bias_scr[:, pl.ds(pl.multiple_of(off + j * kt, LANES), kt)])
                s_scr[slot, m, :, keys[j]] = t
                for c in range(kt // LANES):
                    mx = jnp.maximum(mx, t[:, c * LANES:(c + 1) * LANES])
            m_scr[slot, m] = mx

    def softmax_stage(i, slot):
        rows = pl.ds(pl.multiple_of(i * tq, tq), tq)
        den = []
        for m in range(2):
            mx = jnp.max(m_scr[slot, m], axis=-1, keepdims=True)
            acc = jnp.zeros((tq, LANES), F32)
            for j in range(n_kt):
                p = jnp.exp2(s_scr[slot, m, :, keys[j]] - mx)
                s_scr[slot, m, :, keys[j]] = p
                for c in range(kt // LANES):
                    acc = acc + p[:, c * LANES:(c + 1) * LANES]
            den.append(jnp.sum(acc, axis=-1, keepdims=True))
        ratio = lam * den[0] / den[1]
        o = jnp.zeros((tq, d2), F32)
        for j in range(n_kt):
            a = (s_scr[slot, 0, :, keys[j]] - s_scr[slot, 1, :, keys[j]] * ratio).astype(BF16)
            o = o + _dot(a, v_ref[keys[j], :])
        o = o / den[0]
        y = o * lax.rsqrt(jnp.mean(o * o, axis=-1, keepdims=True) + EPS) * dn_ref[...]
        y = y * (1.0 - lam_init) * _silu(z_ref[rows, :].astype(F32))
        o_ref[rows, :] = y.astype(o_ref.dtype)

    logits_stage(0, 0)
    if n_blocks % 2 == 0:
        def pair(p, carry):
            logits_stage(2 * p + 1, 1)
            softmax_stage(2 * p, 0)
            logits_stage(2 * p + 2, 0)
            softmax_stage(2 * p + 1, 1)
            return carry
        lax.fori_loop(0, n_blocks // 2 - 1, pair, 0)
        logits_stage(n_blocks - 1, 1)
        softmax_stage(n_blocks - 2, 0)
        softmax_stage(n_blocks - 1, 1)
    else:
        assert n_blocks == 1
        softmax_stage(0, 0)


def _attn_call(proj, rel_bias, lam_par, dnorm, lam_init, batch, seq, width, col0):
    m = proj.shape[0]
    nh = DIFF_HEADS
    hw = width // nh
    c0 = col0 // hw
    tq = min(ATTN_Q_ROWS, seq)
    blk = lambda c: pl.BlockSpec((seq, hw), lambda h, b: (b, c + h))
    return pl.pallas_call(
        functools.partial(_attn_kernel, lam_init),
        grid=(nh, batch),
        in_specs=[pl.BlockSpec(memory_space=pltpu.SMEM),
                  pl.BlockSpec(lam_par.shape, lambda h, b: (0, 0)),
                  pl.BlockSpec((1, hw), lambda h, b: (0, 0)),
                  blk(c0), blk(c0 + nh), blk(c0 + 2 * nh), blk(c0 + 3 * nh)],
        out_specs=pl.BlockSpec((seq, hw), lambda h, b: (b, h)),
        out_shape=jax.ShapeDtypeStruct((m, width), BF16),
        scratch_shapes=[pltpu.VMEM((tq, 2 * seq - tq), F32),
                        pltpu.VMEM((2, 2, tq, seq), F32),
                        pltpu.VMEM((2, 2, tq, LANES), F32)],
        compiler_params=_params("arbitrary", "arbitrary"),
        name="diff_attn",
    )(rel_bias, lam_par, dnorm, proj, proj, proj, proj)


def _merge_kernel(h_ref, za_ref, zb_ref, zc_ref, wm0_ref, wm1_ref, wm2_ref,
                  bm0_ref, bm1_ref, bm2_ref, wb0_ref, wb1_ref, wb2_ref, o_ref):
    hh = h_ref[...]
    acc = None
    for z_ref, wm_ref, bm_ref, wb_ref in ((za_ref, wm0_ref, bm0_ref, wb0_ref),
                                          (zb_ref, wm1_ref, bm1_ref, wb1_ref),
                                          (zc_ref, wm2_ref, bm2_ref, wb2_ref)):
        gate = _sigmoid(_dot(hh, wm_ref[...]) + bm_ref[...])
        term = gate * _dot(z_ref[...], wb_ref[0])
        acc = term if acc is None else acc + term
    o_ref[...] = acc.astype(o_ref.dtype)


def _merge_call(h, za, zb, zc, wm, bm, wb):
    m, d = h.shape
    tm = min(512, m)
    tn = min(256, d)
    nj = d // tn
    row = pl.BlockSpec((tm, d), lambda i, j: (i, 0))
    wm_spec = lambda br: pl.BlockSpec((d, tn), lambda i, j: (0, br * nj + j))
    bm_spec = lambda br: pl.BlockSpec((1, tn), lambda i, j: (0, br * nj + j))
    wb_spec = lambda br: pl.BlockSpec((1, d, tn), lambda i, j: (br, 0, j))
    return pl.pallas_call(
        _merge_kernel,
        grid=(m // tm, nj),
        in_specs=[row, row, row, row,
                  wm_spec(0), wm_spec(1), wm_spec(2),
                  bm_spec(0), bm_spec(1), bm_spec(2),
                  wb_spec(0), wb_spec(1), wb_spec(2)],
        out_specs=pl.BlockSpec((tm, tn), lambda i, j: (i, j)),
        out_shape=jax.ShapeDtypeStruct((m, d), BF16),
        compiler_params=_params("parallel", "arbitrary"),
        name="merge",
    )(h, za, zb, zc, wm, wm, wm, bm, bm, bm, wb, wb, wb)


def _out_kernel(mg_ref, w_ref, x_ref, g_ref, o_ref):
    out = _dot(mg_ref[...], w_ref[...])
    y = out * lax.rsqrt(jnp.mean(out * out, axis=-1, keepdims=True) + EPS) * g_ref[...]
    o_ref[...] = x_ref[...] + y


def _out_call(merged, w, x2, g):
    m, d = x2.shape
    tm = min(512, m)
    return pl.pallas_call(
        _out_kernel,
        grid=(m // tm,),
        in_specs=[pl.BlockSpec((tm, d), lambda i: (i, 0)),
                  pl.BlockSpec((d, d), lambda i: (0, 0)),
                  pl.BlockSpec((tm, d), lambda i: (i, 0)),
                  pl.BlockSpec((1, d), lambda i: (0, 0))],
        out_specs=pl.BlockSpec((tm, d), lambda i: (i, 0)),
        out_shape=jax.ShapeDtypeStruct((m, d), F32),
        compiler_params=_params("parallel"),
        name="out_proj",
    )(merged, w, x2, g.reshape(1, d))


def kernel(x, norm_pre, w_in, gmlp_ln_g, gmlp_ln_b, gmlp_ws, gmlp_bs, gla_wa2, gla_ba, gla_norm,
           diff_lambda, diff_norm, rel_bias, w_branch, w_merge, b_merge, w_out, norm_post):
    batch, seq, d = x.shape
    depth = norm_pre.shape[0]
    width = d
    m = batch * seq
    hk = gla_wa2.shape[-1]
    lr_col = 3 * width + 2 * hk + 2 * width
    lr_w = 2 * GLA_RANK
    assert seq % GLA_CHUNK == 0 and seq % ATTN_Q_ROWS == 0 and seq % GMLP_CHUNK == 0
    assert w_in.shape[-1] == lr_col + lr_w + 4 * width

    cmats = jnp.asarray(_gla_constants(), BF16)
    x2 = x.reshape(m, d)
    for l in range(depth):
        w_main = jnp.concatenate([w_in[l, :, :lr_col], w_in[l, :, lr_col + lr_w:]], axis=1).astype(BF16)
        w_lr = jnp.pad(w_in[l, :, lr_col:lr_col + lr_w], ((0, 0), (0, LANES - lr_w))).astype(BF16)
        wa = jnp.zeros((2, LANES, hk), F32)
        wa = wa.at[0, :GLA_RANK].set(gla_wa2[l, 0]).at[1, GLA_RANK:lr_w].set(gla_wa2[l, 1]).astype(BF16)
        ba = gla_ba[l].reshape(2, 1, hk)

        h = _rms_call(x2, norm_pre[l])
        proj = _mm_call(h, w_main, "in_proj")
        lr = _mm_call(h, w_lr, "lr_proj")

        za = _gmlp_call(proj, gmlp_ln_g[l].reshape(1, width), gmlp_ln_b[l].reshape(1, width),
                        gmlp_ws[l].astype(BF16), gmlp_bs[l].T, width)
        zb = _gla_call(proj, lr, wa, ba, gla_norm[l].reshape(1, -1), cmats, batch, seq, width,
                       3 * width)
        lam_init = 0.8 - 0.6 * math.exp(-0.3 * l)
        zc = _attn_call(proj, rel_bias, diff_lambda[l], diff_norm[l].reshape(1, -1), lam_init,
                        batch, seq, width, lr_col)
        merged = _merge_call(h, za, zb, zc, w_merge[l].astype(BF16), b_merge[l].reshape(1, -1),
                             w_branch[l].astype(BF16))
        x2 = _out_call(merged, w_out[l].astype(BF16), x2, norm_post[l])
    return x2.reshape(batch, seq, d)
```

```python
import functools
import math

import numpy as np
import jax
import jax.numpy as jnp
from jax import lax
from jax.experimental import pallas as pl
from jax.experimental.pallas import tpu as pltpu

F32 = jnp.float32
BF16 = jnp.bfloat16

EPS = 1e-6
LOG2E = math.log2(math.e)
N_BRANCH = 3
GMLP_CHUNK = 128
GMLP_GROUPS = 8
GLA_HEADS = 4
GLA_RANK = 16
GLA_TAU = 16.0
GLA_SUB = 64
GLA_CHUNK = 2 * GLA_SUB
GLA_GROUP = 4
DIFF_HEADS = 8
ATTN_Q_ROWS = 256
ATTN_KEY_TILE = 256
REL_BUCKETS = 32
T5_STARTS = (1, 2, 3, 4, 5, 6, 7, 8, 12, 16, 23, 32, 46, 64, 91)

LANES = 128
VMEM_LIMIT = 52 * 1024 * 1024


def _params(*sem):
    return pltpu.CompilerParams(dimension_semantics=sem, vmem_limit_bytes=VMEM_LIMIT)


def _gelu(x):
    return 0.5 * x * (1.0 + jnp.tanh(math.sqrt(2.0 / math.pi) * (x + 0.044715 * (x * x * x))))


def _sigmoid(x):
    return 1.0 / (1.0 + jnp.exp(-x))


def _silu(x):
    return x * _sigmoid(x)


def _log_sigmoid(x):
    return jnp.minimum(x, 0.0) - jnp.log1p(jnp.exp(-jnp.abs(x)))


def _dot(a, b):
    return jnp.dot(a, b, preferred_element_type=F32)


def _dot_nt(a, b):
    return lax.dot_general(a, b, (((1,), (1,)), ((), ())), preferred_element_type=F32)


def _rms_kernel(x_ref, g_ref, o_ref):
    x = x_ref[...]
    y = x * lax.rsqrt(jnp.mean(x * x, axis=-1, keepdims=True) + EPS)
    o_ref[...] = (y * g_ref[...]).astype(o_ref.dtype)


def _rms_call(x2, g):
    m, d = x2.shape
    tm = min(512, m)
    return pl.pallas_call(
        _rms_kernel,
        grid=(m // tm,),
        in_specs=[pl.BlockSpec((tm, d), lambda i: (i, 0)),
                  pl.BlockSpec((1, d), lambda i: (0, 0))],
        out_specs=pl.BlockSpec((tm, d), lambda i: (i, 0)),
        out_shape=jax.ShapeDtypeStruct((m, d), BF16),
        compiler_params=_params("parallel"),
        name="rms_pre",
    )(x2, g.reshape(1, d))


def _mm_kernel(a_ref, w_ref, o_ref, w_scr):
    @pl.when(pl.program_id(1) == 0)
    def _():
        w_scr[...] = w_ref[...].astype(BF16)
    o_ref[...] = _dot(a_ref[...], w_scr[...]).astype(o_ref.dtype)


def _mm_call(a, w, layer, n, name):
    m, k = a.shape
    tm = min(1024, m)
    tn = min(1024, n)
    return pl.pallas_call(
        _mm_kernel,
        grid=(n // tn, m // tm),
        in_specs=[pl.BlockSpec((tm, k), lambda j, i: (i, 0)),
                  pl.BlockSpec((None, k, tn), lambda j, i: (layer, 0, j))],
        out_specs=pl.BlockSpec((tm, tn), lambda j, i: (i, j)),
        out_shape=jax.ShapeDtypeStruct((m, n), BF16),
        scratch_shapes=[pltpu.VMEM((k, tn), BF16)],
        compiler_params=_params("arbitrary", "arbitrary"),
        name=name,
    )(a, w)


def _gmlp_kernel(u_ref, v_ref, z_ref, lng_ref, lnb_ref, ws_ref, bs_ref, o_ref):
    tm, w = v_ref.shape
    gc = w // GMLP_GROUPS
    gv = _gelu(v_ref[...].astype(F32))
    mu = jnp.mean(gv, axis=-1, keepdims=True)
    cen = gv - mu
    var = jnp.mean(cen * cen, axis=-1, keepdims=True)
    sv = (cen * lax.rsqrt(var + EPS) * lng_ref[...] + lnb_ref[...]).astype(BF16)
    for c in range(tm // GMLP_CHUNK):
        rows = slice(c * GMLP_CHUNK, (c + 1) * GMLP_CHUNK)
        for g in range(GMLP_GROUPS):
            cols = slice(g * gc, (g + 1) * gc)
            mixed = _dot(ws_ref[g], sv[rows, cols]) + bs_ref[:, g:g + 1]
            u = _gelu(u_ref[rows, cols].astype(F32))
            z = _silu(z_ref[rows, cols].astype(F32))
            o_ref[rows, cols] = (u * mixed * z).astype(o_ref.dtype)


def _gmlp_call(proj, lng, lnb, ws, bs_t, layer, width):
    m = proj.shape[0]
    tm = min(256, m)
    blk = lambda c: pl.BlockSpec((tm, width), lambda i: (i, c))
    full = lambda a: pl.BlockSpec(a.shape, lambda i: (0,) * a.ndim)
    return pl.pallas_call(
        _gmlp_kernel,
        grid=(m // tm,),
        in_specs=[blk(0), blk(1), blk(2), full(lng), full(lnb),
                  pl.BlockSpec((None,) + ws.shape[1:], lambda i: (layer, 0, 0, 0)), full(bs_t)],
        out_specs=pl.BlockSpec((tm, width), lambda i: (i, 0)),
        out_shape=jax.ShapeDtypeStruct((m, width), BF16),
        compiler_params=_params("parallel"),
        name="gmlp",
    )(proj, proj, proj, lng, lnb, ws, bs_t)


def _gla_constants():
    c, s = GLA_CHUNK, GLA_SUB
    i = np.arange(c)[:, None]
    j = np.arange(c)[None, :]
    same = (i // s) == (j // s)
    mats = []
    for rev in (False, True):
        if not rev:
            cum = same & (j <= i)
            ref = same & ((j % s) <= s // 2 - 1)
        else:
            cum = same & (j >= i)
            ref = same & ((j % s) >= s // 2)
        cum, ref, tot = (t.astype(np.float32) for t in (cum, ref, same))
        blocks = np.concatenate([cum, cum - ref, ref - cum, tot - cum], axis=0)
        mats.append(np.concatenate([blocks, blocks], axis=1))
    return np.stack(mats)


def _gla_kernel(q_ref, k_ref, v_ref, z_ref, lr_ref, wa_ref, ba_ref, gn_ref, cm_ref, o_ref,
                vt_scr, st_scr, o_scr):
    s_len, dk = q_ref.shape
    c, sub = GLA_CHUNK, GLA_SUB
    n_chunks = s_len // c

    vt_scr[...] = v_ref[...].astype(F32).T.astype(BF16)
    st_scr[...] = jnp.zeros_like(st_scr)
    o_scr[...] = jnp.zeros_like(o_scr)

    ri = lax.broadcasted_iota(jnp.int32, (c, c), 0)
    ci = lax.broadcasted_iota(jnp.int32, (c, c), 1)
    same = (ri // sub) == (ci // sub)
    row_in_sub0 = lax.broadcasted_iota(jnp.int32, (c, 1), 0) < sub

    n_grp = min(GLA_GROUP, n_chunks)
    assert n_chunks % n_grp == 0

    def group(d, gi):
        rev = d == 1
        cs = range(n_grp)
        base = pl.multiple_of(gi * (n_grp * c), n_grp * c)
        rows = pl.ds(base, n_grp * c)
        crow = [pl.ds(pl.multiple_of(base + i * c, c), c) for i in cs]
        part = lambda x, i: x[i * c:(i + 1) * c]
        q = q_ref[rows, :].astype(F32) * (dk ** -0.5)
        k = k_ref[rows, :].astype(F32)
        g = _log_sigmoid(_dot(lr_ref[rows, :], wa_ref[d]) + ba_ref[d]) * (LOG2E / GLA_TAU)
        g_hi = g.astype(BF16)
        g_lo = (g - g_hi.astype(F32)).astype(BF16)
        e = [_dot(cm_ref[d], jnp.concatenate([part(g_hi, i), part(g_lo, i)], axis=0)) for i in cs]
        cum = [e[i][:c] for i in cs]
        if rev:
            last0, last1 = [x[0:1] for x in cum], [x[sub:sub + 1] for x in cum]
            diag_mask = same & (ci >= ri)
            off_mask = (ri < sub) & (ci >= sub)
        else:
            last0, last1 = [x[sub - 1:sub] for x in cum], [x[c - 1:c] for x in cum]
            diag_mask = same & (ci <= ri)
            off_mask = (ri >= sub) & (ci < sub)
        q_in = [(part(q, i) * jnp.exp2(e[i][c:2 * c])).astype(BF16) for i in cs]
        k_in = [(part(k, i) * jnp.exp2(e[i][2 * c:3 * c])).astype(BF16) for i in cs]
        q_dec = [part(q, i) * jnp.exp2(cum[i]) for i in cs]
        k_dec = [part(k, i) * jnp.exp2(e[i][3 * c:]) for i in cs]
        s_diag = [_dot_nt(q_in[i], k_in[i]) for i in cs]
        s_off = [_dot_nt(q_dec[i].astype(BF16), k_dec[i].astype(BF16)) for i in cs]
        scores = [(jnp.where(diag_mask, s_diag[i], 0.0)
                   + jnp.where(off_mask, s_off[i], 0.0)).astype(BF16) for i in cs]
        o_intra = [_dot(scores[i], v_ref[crow[i], :]) for i in cs]
        if rev:
            q_st = [q_dec[i] * jnp.where(row_in_sub0, jnp.exp2(last1[i]), 1.0) for i in cs]
            k_st = [k_dec[i] * jnp.where(row_in_sub0, 1.0, jnp.exp2(last0[i])) for i in cs]
        else:
            q_st = [q_dec[i] * jnp.where(row_in_sub0, 1.0, jnp.exp2(last0[i])) for i in cs]
            k_st = [k_dec[i] * jnp.where(row_in_sub0, jnp.exp2(last1[i]), 1.0) for i in cs]
        kv_t = [_dot(vt_scr[:, crow[i]], k_st[i].astype(BF16)) for i in cs]
        st = st_scr[d]
        seen = [None] * n_grp
        for i in (reversed(cs) if rev else cs):
            seen[i] = st.astype(BF16)
            st = st * jnp.exp2(last0[i] + last1[i]) + kv_t[i]
        st_scr[d] = st
        for i in cs:
            o_scr[crow[i], :] += o_intra[i] + _dot_nt(q_st[i].astype(BF16), seen[i])

    def body(t, carry):
        group(0, t)
        group(1, n_chunks // n_grp - 1 - t)
        return carry

    lax.fori_loop(0, n_chunks // n_grp, body, 0)

    o = o_scr[...]
    y = o * lax.rsqrt(jnp.mean(o * o, axis=-1, keepdims=True) + EPS) * gn_ref[...]
    o_ref[...] = (y * _silu(z_ref[...].astype(F32))).astype(o_ref.dtype)


def _gla_call(proj, lr, wa, ba, gnorm, cmats, batch, seq, width, col0):
    m = proj.shape[0]
    h = GLA_HEADS
    dv = width // h
    dk = dv // 2
    qb, kb = col0 // dk, col0 // dk + h
    vb, zb = (col0 + 2 * h * dk) // dv, (col0 + 2 * h * dk) // dv + h
    return pl.pallas_call(
        _gla_kernel,
        grid=(batch, h),
        in_specs=[pl.BlockSpec((seq, dk), lambda b, i: (b, qb + i)),
                  pl.BlockSpec((seq, dk), lambda b, i: (b, kb + i)),
                  pl.BlockSpec((seq, dv), lambda b, i: (b, vb + i)),
                  pl.BlockSpec((seq, dv), lambda b, i: (b, zb + i)),
                  pl.BlockSpec((seq, LANES), lambda b, i: (b, 0)),
                  pl.BlockSpec((2, LANES, dk), lambda b, i: (0, 0, i)),
                  pl.BlockSpec((2, 1, dk), lambda b, i: (0, 0, i)),
                  pl.BlockSpec((1, dv), lambda b, i: (0, 0)),
                  pl.BlockSpec(cmats.shape, lambda b, i: (0, 0, 0))],
        out_specs=pl.BlockSpec((seq, dv), lambda b, i: (b, i)),
        out_shape=jax.ShapeDtypeStruct((m, width), BF16),
        scratch_shapes=[pltpu.VMEM((dv, seq), BF16),
                        pltpu.VMEM((2, dv, dk), F32),
                        pltpu.VMEM((seq, dv), F32)],
        compiler_params=_params("parallel", "parallel"),
        name="gla",
    )(proj, proj, proj, proj, lr, wa, ba, gnorm, cmats)


def _attn_kernel(lam_init, relb_ref, lam_ref, dn_ref, q_ref, k_ref, v_ref, z_ref, o_ref,
                 bias_scr, s_scr, m_scr):
    h = pl.program_id(0)
    s_len, d2 = q_ref.shape
    d = d2 // 2
    tq = bias_scr.shape[0]
    kt = min(ATTN_KEY_TILE, s_len)
    n_kt = s_len // kt
    n_tiles = bias_scr.shape[1] // LANES

    @pl.when(pl.program_id(1) == 0)
    def _():
        def tile(t, carry):
            c0 = pl.multiple_of(t * LANES, LANES)
            rel = (lax.broadcasted_iota(jnp.int32, (tq, LANES), 1) + (c0 - (s_len - tq))
                   - lax.broadcasted_iota(jnp.int32, (tq, LANES), 0))
            dist = jnp.abs(rel)
            neg = jnp.full((tq, LANES), relb_ref[0, h], F32)
            pos = jnp.full((tq, LANES), relb_ref[REL_BUCKETS // 2, h], F32)
            for bucket, start in enumerate(T5_STARTS, start=1):
                far = dist >= start
                neg = jnp.where(far, relb_ref[bucket, h], neg)
                pos = jnp.where(far, relb_ref[REL_BUCKETS // 2 + bucket, h], pos)
            bias_scr[:, pl.ds(c0, LANES)] = jnp.where(rel > 0, pos, neg) * LOG2E
            return carry
        lax.fori_loop(0, n_tiles, tile, 0)

    lv = lam_ref[...]
    lam = (jnp.exp(jnp.sum(lv[0:1] * lv[1:2], axis=-1, keepdims=True))
           - jnp.exp(jnp.sum(lv[2:3] * lv[3:4], axis=-1, keepdims=True)) + lam_init)

    keys = [slice(j * kt, (j + 1) * kt) for j in range(n_kt)]
    n_blocks = s_len // tq

    def logits_stage(i, slot):
        r0 = pl.multiple_of(i * tq, tq)
        off = s_len - tq - r0
        q = (q_ref[pl.ds(r0, tq), :].astype(F32) * (d ** -0.5 * LOG2E)).astype(BF16)
        for m in range(2):
            mx = jnp.full((tq, LANES), -jnp.inf, F32)
            for j in range(n_kt):
                t = (_dot_nt(q[:, m * d:(m + 1) * d], k_ref[keys[j], m * d:(m + 1) * d])
                     + bias_scr[:, pl.ds(pl.multiple_of(off + j * kt, LANES), kt)])
                s_scr[slot, m, :, keys[j]] = t
                for c in range(kt // LANES):
                    mx = jnp.maximum(mx, t[:, c * LANES:(c + 1) * LANES])
            m_scr[slot, m] = mx

    def softmax_stage(i, slot):
        rows = pl.ds(pl.multiple_of(i * tq, tq), tq)
        den = []
        for m in range(2):
            mx = jnp.max(m_scr[slot, m], axis=-1, keepdims=True)
            acc = jnp.zeros((tq, LANES), F32)
            for j in range(n_kt):
                p = jnp.exp2(s_scr[slot, m, :, keys[j]] - mx)
                s_scr[slot, m, :, keys[j]] = p
                for c in range(kt // LANES):
                    acc = acc + p[:, c * LANES:(c + 1) * LANES]
            den.append(jnp.sum(acc, axis=-1, keepdims=True))
        ratio = lam * den[0] / den[1]
        o = jnp.zeros((tq, d2), F32)
        for j in range(n_kt):
            a = (s_scr[slot, 0, :, keys[j]] - s_scr[slot, 1, :, keys[j]] * ratio).astype(BF16)
            o = o + _dot(a, v_ref[keys[j], :])
        o = o / den[0]
        y = o * lax.rsqrt(jnp.mean(o * o, axis=-1, keepdims=True) + EPS) * dn_ref[...]
        y = y * (1.0 - lam_init) * _silu(z_ref[rows, :].astype(F32))
        o_ref[rows, :] = y.astype(o_ref.dtype)

    logits_stage(0, 0)
    if n_blocks % 2 == 0:
        def pair(p, carry):
            logits_stage(2 * p + 1, 1)
            softmax_stage(2 * p, 0)
            logits_stage(2 * p + 2, 0)
            softmax_stage(2 * p + 1, 1)
            return carry
        lax.fori_loop(0, n_blocks // 2 - 1, pair, 0)
        logits_stage(n_blocks - 1, 1)
        softmax_stage(n_blocks - 2, 0)
        softmax_stage(n_blocks - 1, 1)
    else:
        assert n_blocks == 1
        softmax_stage(0, 0)


def _attn_call(proj, rel_bias, lam_par, dnorm, lam_init, batch, seq, width, col0):
    m = proj.shape[0]
    nh = DIFF_HEADS
    hw = width // nh
    c0 = col0 // hw
    tq = min(ATTN_Q_ROWS, seq)
    blk = lambda c: pl.BlockSpec((seq, hw), lambda h, b: (b, c + h))
    return pl.pallas_call(
        functools.partial(_attn_kernel, lam_init),
        grid=(nh, batch),
        in_specs=[pl.BlockSpec(memory_space=pltpu.SMEM),
                  pl.BlockSpec(lam_par.shape, lambda h, b: (0, 0)),
                  pl.BlockSpec((1, hw), lambda h, b: (0, 0)),
                  blk(c0), blk(c0 + nh), blk(c0 + 2 * nh), blk(c0 + 3 * nh)],
        out_specs=pl.BlockSpec((seq, hw), lambda h, b: (b, h)),
        out_shape=jax.ShapeDtypeStruct((m, width), BF16),
        scratch_shapes=[pltpu.VMEM((tq, 2 * seq - tq), F32),
                        pltpu.VMEM((2, 2, tq, seq), F32),
                        pltpu.VMEM((2, 2, tq, LANES), F32)],
        compiler_params=_params("arbitrary", "arbitrary"),
        name="diff_attn",
    )(rel_bias, lam_par, dnorm, proj, proj, proj, proj)


def _merge_kernel(h_ref, za_ref, zb_ref, zc_ref, wm0_ref, wm1_ref, wm2_ref,
                  bm0_ref, bm1_ref, bm2_ref, wb0_ref, wb1_ref, wb2_ref, o_ref):
    hh = h_ref[...]
    acc = None
    for z_ref, wm_ref, bm_ref, wb_ref in ((za_ref, wm0_ref, bm0_ref, wb0_ref),
                                          (zb_ref, wm1_ref, bm1_ref, wb1_ref),
                                          (zc_ref, wm2_ref, bm2_ref, wb2_ref)):
        gate = _sigmoid(_dot(hh, wm_ref[...]) + bm_ref[...])
        term = gate * _dot(z_ref[...], wb_ref[0])
        acc = term if acc is None else acc + term
    o_ref[...] = acc.astype(o_ref.dtype)


def _merge_call(h, za, zb, zc, wm, bm, wb, layer):
    m, d = h.shape
    tm = min(512, m)
    tn = min(256, d)
    nj = d // tn
    row = pl.BlockSpec((tm, d), lambda i, j: (i, 0))
    wm_spec = lambda br: pl.BlockSpec((None, d, tn), lambda i, j: (layer, 0, br * nj + j))
    bm_spec = lambda br: pl.BlockSpec((None, 1, tn), lambda i, j: (layer, 0, br * nj + j))
    wb_spec = lambda br: pl.BlockSpec((None, 1, d, tn), lambda i, j: (layer, br, 0, j))
    return pl.pallas_call(
        _merge_kernel,
        grid=(m // tm, nj),
        in_specs=[row, row, row, row,
                  wm_spec(0), wm_spec(1), wm_spec(2),
                  bm_spec(0), bm_spec(1), bm_spec(2),
                  wb_spec(0), wb_spec(1), wb_spec(2)],
        out_specs=pl.BlockSpec((tm, tn), lambda i, j: (i, j)),
        out_shape=jax.ShapeDtypeStruct((m, d), BF16),
        compiler_params=_params("parallel", "arbitrary"),
        name="merge",
    )(h, za, zb, zc, wm, wm, wm, bm, bm, bm, wb, wb, wb)


def _out_kernel(mg_ref, w_ref, x_ref, g_ref, o_ref):
    out = _dot(mg_ref[...], w_ref[...])
    y = out * lax.rsqrt(jnp.mean(out * out, axis=-1, keepdims=True) + EPS) * g_ref[...]
    o_ref[...] = x_ref[...] + y


def _out_call(merged, w, layer, x2, g):
    m, d = x2.shape
    tm = min(512, m)
    return pl.pallas_call(
        _out_kernel,
        grid=(m // tm,),
        in_specs=[pl.BlockSpec((tm, d), lambda i: (i, 0)),
                  pl.BlockSpec((None, d, d), lambda i: (layer, 0, 0)),
                  pl.BlockSpec((tm, d), lambda i: (i, 0)),
                  pl.BlockSpec((1, d), lambda i: (0, 0))],
        out_specs=pl.BlockSpec((tm, d), lambda i: (i, 0)),
        out_shape=jax.ShapeDtypeStruct((m, d), F32),
        compiler_params=_params("parallel"),
        name="out_proj",
    )(merged, w, x2, g.reshape(1, d))


def kernel(x, norm_pre, w_in, gmlp_ln_g, gmlp_ln_b, gmlp_ws, gmlp_bs, gla_wa2, gla_ba, gla_norm,
           diff_lambda, diff_norm, rel_bias, w_branch, w_merge, b_merge, w_out, norm_post):
    batch, seq, d = x.shape
    depth = norm_pre.shape[0]
    width = d
    m = batch * seq
    hk = gla_wa2.shape[-1]
    lr_col = 3 * width + 2 * hk + 2 * width
    lr_w = 2 * GLA_RANK
    assert seq % GLA_CHUNK == 0 and seq % ATTN_Q_ROWS == 0 and seq % GMLP_CHUNK == 0
    assert w_in.shape[-1] == lr_col + lr_w + 4 * width

    cmats = jnp.asarray(_gla_constants(), BF16)
    w_c = w_in[:, :, lr_col + lr_w:].astype(BF16)
    w_lr = jnp.pad(w_in[:, :, lr_col:lr_col + lr_w], ((0, 0), (0, 0), (0, LANES - lr_w))).astype(BF16)
    ws_b = gmlp_ws.astype(BF16)
    wm_b = w_merge.astype(BF16)
    wb_b = w_branch.astype(BF16)
    wo_b = w_out.astype(BF16)
    bm = b_merge.reshape(depth, 1, -1)

    x2 = x.reshape(m, d)
    for l in range(depth):
        wa = jnp.zeros((2, LANES, hk), F32)
        wa = wa.at[0, :GLA_RANK].set(gla_wa2[l, 0]).at[1, GLA_RANK:lr_w].set(gla_wa2[l, 1]).astype(BF16)
        ba = gla_ba[l].reshape(2, 1, hk)

        h = _rms_call(x2, norm_pre[l])
        proj_ab = _mm_call(h, w_in, l, lr_col, "in_proj_ab")
        proj_c = _mm_call(h, w_c, l, 4 * width, "in_proj_c")
        lr = _mm_call(h, w_lr, l, LANES, "lr_proj")

        za = _gmlp_call(proj_ab, gmlp_ln_g[l].reshape(1, width), gmlp_ln_b[l].reshape(1, width),
                        ws_b, gmlp_bs[l].T, l, width)
        zb = _gla_call(proj_ab, lr, wa, ba, gla_norm[l].reshape(1, -1), cmats, batch, seq, width,
                       3 * width)
        lam_init = 0.8 - 0.6 * math.exp(-0.3 * l)
        zc = _attn_call(proj_c, rel_bias, diff_lambda[l], diff_norm[l].reshape(1, -1), lam_init,
                        batch, seq, width, 0)
        merged = _merge_call(h, za, zb, zc, wm_b, bm, wb_b, l)
        x2 = _out_call(merged, wo_b, l, x2, norm_post[l])
    return x2.reshape(batch, seq, d)
```

```python
import functools
import math

import numpy as np
import jax
import jax.numpy as jnp
from jax import lax
from jax.experimental import pallas as pl
from jax.experimental.pallas import tpu as pltpu

F32 = jnp.float32
BF16 = jnp.bfloat16

EPS = 1e-6
LOG2E = math.log2(math.e)
N_BRANCH = 3
GMLP_CHUNK = 128
GMLP_GROUPS = 8
GLA_HEADS = 4
GLA_RANK = 16
GLA_TAU = 16.0
GLA_SUB = 64
GLA_CHUNK = 2 * GLA_SUB
GLA_GROUP = 4
DIFF_HEADS = 8
ATTN_Q_ROWS = 256
ATTN_KEY_TILE = 256
REL_BUCKETS = 32
T5_STARTS = (1, 2, 3, 4, 5, 6, 7, 8, 12, 16, 23, 32, 46, 64, 91)

LANES = 128
SUBLANES = 8
VMEM_LIMIT = 52 * 1024 * 1024


def _params(*sem):
    return pltpu.CompilerParams(dimension_semantics=sem, vmem_limit_bytes=VMEM_LIMIT)


def _gelu(x):
    return 0.5 * x * (1.0 + jnp.tanh(math.sqrt(2.0 / math.pi) * (x + 0.044715 * (x * x * x))))


def _sigmoid(x):
    return 1.0 / (1.0 + jnp.exp(-x))


def _silu(x):
    return x * _sigmoid(x)


def _log_sigmoid(x):
    return jnp.minimum(x, 0.0) - jnp.log1p(jnp.exp(-jnp.abs(x)))


def _dot(a, b):
    return jnp.dot(a, b, preferred_element_type=F32)


def _dot_nt(a, b):
    return lax.dot_general(a, b, (((1,), (1,)), ((), ())), preferred_element_type=F32)


def _rms_kernel(x_ref, g_ref, o_ref):
    x = x_ref[...]
    y = x * lax.rsqrt(jnp.mean(x * x, axis=-1, keepdims=True) + EPS)
    o_ref[...] = (y * g_ref[...]).astype(o_ref.dtype)


def _rms_call(x2, g):
    m, d = x2.shape
    tm = min(512, m)
    return pl.pallas_call(
        _rms_kernel,
        grid=(m // tm,),
        in_specs=[pl.BlockSpec((tm, d), lambda i: (i, 0)),
                  pl.BlockSpec((1, d), lambda i: (0, 0))],
        out_specs=pl.BlockSpec((tm, d), lambda i: (i, 0)),
        out_shape=jax.ShapeDtypeStruct((m, d), BF16),
        compiler_params=_params("parallel"),
        name="rms_pre",
    )(x2, g.reshape(1, d))


def _mm_kernel(a_ref, w_ref, o_ref, w_scr):
    @pl.when(pl.program_id(1) == 0)
    def _():
        w_scr[...] = w_ref[0].astype(BF16)
    o_ref[...] = _dot_nt(a_ref[...], w_scr[...]).astype(o_ref.dtype)


def _mm_call(a, w_t, layer, row0, n, name):
    m, k = a.shape
    tm = min(1024, m)
    tn = min(1024, n)
    return pl.pallas_call(
        _mm_kernel,
        grid=(n // tn, m // tm),
        in_specs=[pl.BlockSpec((tm, k), lambda j, i: (i, 0)),
                  pl.BlockSpec((pl.Element(1), pl.Element(tn), pl.Element(k)),
                               lambda j, i: (layer, pl.multiple_of(row0 + j * tn, SUBLANES), 0))],
        out_specs=pl.BlockSpec((tm, tn), lambda j, i: (i, j)),
        out_shape=jax.ShapeDtypeStruct((m, n), BF16),
        scratch_shapes=[pltpu.VMEM((tn, k), BF16)],
        compiler_params=_params("arbitrary", "arbitrary"),
        name=name,
    )(a, w_t)


def _gmlp_kernel(u_ref, v_ref, z_ref, lng_ref, lnb_ref, ws_ref, bs_ref, o_ref):
    tm, w = v_ref.shape
    gc = w // GMLP_GROUPS
    gv = _gelu(v_ref[...].astype(F32))
    mu = jnp.mean(gv, axis=-1, keepdims=True)
    cen = gv - mu
    var = jnp.mean(cen * cen, axis=-1, keepdims=True)
    sv = (cen * lax.rsqrt(var + EPS) * lng_ref[...] + lnb_ref[...]).astype(BF16)
    for c in range(tm // GMLP_CHUNK):
        rows = slice(c * GMLP_CHUNK, (c + 1) * GMLP_CHUNK)
        for g in range(GMLP_GROUPS):
            cols = slice(g * gc, (g + 1) * gc)
            mixed = _dot(ws_ref[g], sv[rows, cols]) + bs_ref[:, g:g + 1]
            u = _gelu(u_ref[rows, cols].astype(F32))
            z = _silu(z_ref[rows, cols].astype(F32))
            o_ref[rows, cols] = (u * mixed * z).astype(o_ref.dtype)


def _gmlp_call(proj, lng, lnb, ws, bs_t, layer, width):
    m = proj.shape[0]
    tm = min(256, m)
    blk = lambda c: pl.BlockSpec((tm, width), lambda i: (i, c))
    full = lambda a: pl.BlockSpec(a.shape, lambda i: (0,) * a.ndim)
    return pl.pallas_call(
        _gmlp_kernel,
        grid=(m // tm,),
        in_specs=[blk(0), blk(1), blk(2), full(lng), full(lnb),
                  pl.BlockSpec((None,) + ws.shape[1:], lambda i: (layer, 0, 0, 0)), full(bs_t)],
        out_specs=pl.BlockSpec((tm, width), lambda i: (i, 0)),
        out_shape=jax.ShapeDtypeStruct((m, width), BF16),
        compiler_params=_params("parallel"),
        name="gmlp",
    )(proj, proj, proj, lng, lnb, ws, bs_t)


def _gla_constants():
    c, s = GLA_CHUNK, GLA_SUB
    i = np.arange(c)[:, None]
    j = np.arange(c)[None, :]
    same = (i // s) == (j // s)
    mats = []
    for rev in (False, True):
        if not rev:
            cum = same & (j <= i)
            ref = same & ((j % s) <= s // 2 - 1)
        else:
            cum = same & (j >= i)
            ref = same & ((j % s) >= s // 2)
        cum, ref, tot = (t.astype(np.float32) for t in (cum, ref, same))
        blocks = np.concatenate([cum, cum - ref, ref - cum, tot - cum], axis=0)
        mats.append(np.concatenate([blocks, blocks], axis=1))
    return np.stack(mats)


def _gla_kernel(q_ref, k_ref, v_ref, z_ref, lr_ref, wa_ref, ba_ref, gn_ref, cm_ref, o_ref,
                vt_scr, st_scr, o_scr):
    s_len, dk = q_ref.shape
    c, sub = GLA_CHUNK, GLA_SUB
    n_chunks = s_len // c

    vt_scr[...] = v_ref[...].astype(F32).T.astype(BF16)
    st_scr[...] = jnp.zeros_like(st_scr)
    o_scr[...] = jnp.zeros_like(o_scr)

    ri = lax.broadcasted_iota(jnp.int32, (c, c), 0)
    ci = lax.broadcasted_iota(jnp.int32, (c, c), 1)
    same = (ri // sub) == (ci // sub)
    row_in_sub0 = lax.broadcasted_iota(jnp.int32, (c, 1), 0) < sub

    n_grp = min(GLA_GROUP, n_chunks)
    assert n_chunks % n_grp == 0

    def group(d, gi):
        rev = d == 1
        cs = range(n_grp)
        base = pl.multiple_of(gi * (n_grp * c), n_grp * c)
        rows = pl.ds(base, n_grp * c)
        crow = [pl.ds(pl.multiple_of(base + i * c, c), c) for i in cs]
        part = lambda x, i: x[i * c:(i + 1) * c]
        q = q_ref[rows, :].astype(F32) * (dk ** -0.5)
        k = k_ref[rows, :].astype(F32)
        g = _log_sigmoid(_dot(lr_ref[rows, :], wa_ref[d]) + ba_ref[d]) * (LOG2E / GLA_TAU)
        g_hi = g.astype(BF16)
        g_lo = (g - g_hi.astype(F32)).astype(BF16)
        e = [_dot(cm_ref[d], jnp.concatenate([part(g_hi, i), part(g_lo, i)], axis=0)) for i in cs]
        cum = [e[i][:c] for i in cs]
        if rev:
            last0, last1 = [x[0:1] for x in cum], [x[sub:sub + 1] for x in cum]
            diag_mask = same & (ci >= ri)
            off_mask = (ri < sub) & (ci >= sub)
        else:
            last0, last1 = [x[sub - 1:sub] for x in cum], [x[c - 1:c] for x in cum]
            diag_mask = same & (ci <= ri)
            off_mask = (ri >= sub) & (ci < sub)
        q_in = [(part(q, i) * jnp.exp2(e[i][c:2 * c])).astype(BF16) for i in cs]
        k_in = [(part(k, i) * jnp.exp2(e[i][2 * c:3 * c])).astype(BF16) for i in cs]
        q_dec = [part(q, i) * jnp.exp2(cum[i]) for i in cs]
        k_dec = [part(k, i) * jnp.exp2(e[i][3 * c:]) for i in cs]
        s_diag = [_dot_nt(q_in[i], k_in[i]) for i in cs]
        s_off = [_dot_nt(q_dec[i].astype(BF16), k_dec[i].astype(BF16)) for i in cs]
        scores = [(jnp.where(diag_mask, s_diag[i], 0.0)
                   + jnp.where(off_mask, s_off[i], 0.0)).astype(BF16) for i in cs]
        o_intra = [_dot(scores[i], v_ref[crow[i], :]) for i in cs]
        if rev:
            q_st = [q_dec[i] * jnp.where(row_in_sub0, jnp.exp2(last1[i]), 1.0) for i in cs]
            k_st = [k_dec[i] * jnp.where(row_in_sub0, 1.0, jnp.exp2(last0[i])) for i in cs]
        else:
            q_st = [q_dec[i] * jnp.where(row_in_sub0, 1.0, jnp.exp2(last0[i])) for i in cs]
            k_st = [k_dec[i] * jnp.where(row_in_sub0, jnp.exp2(last1[i]), 1.0) for i in cs]
        kv_t = [_dot(vt_scr[:, crow[i]], k_st[i].astype(BF16)) for i in cs]
        st = st_scr[d]
        seen = [None] * n_grp
        for i in (reversed(cs) if rev else cs):
            seen[i] = st.astype(BF16)
            st = st * jnp.exp2(last0[i] + last1[i]) + kv_t[i]
        st_scr[d] = st
        for i in cs:
            o_scr[crow[i], :] += o_intra[i] + _dot_nt(q_st[i].astype(BF16), seen[i])

    def body(t, carry):
        group(0, t)
        group(1, n_chunks // n_grp - 1 - t)
        return carry

    lax.fori_loop(0, n_chunks // n_grp, body, 0)

    o = o_scr[...]
    y = o * lax.rsqrt(jnp.mean(o * o, axis=-1, keepdims=True) + EPS) * gn_ref[...]
    o_ref[...] = (y * _silu(z_ref[...].astype(F32))).astype(o_ref.dtype)


def _gla_call(proj, lr, wa, ba, gnorm, cmats, batch, seq, width, col0):
    m = proj.shape[0]
    h = GLA_HEADS
    dv = width // h
    dk = dv // 2
    qb, kb = col0 // dk, col0 // dk + h
    vb, zb = (col0 + 2 * h * dk) // dv, (col0 + 2 * h * dk) // dv + h
    return pl.pallas_call(
        _gla_kernel,
        grid=(batch, h),
        in_specs=[pl.BlockSpec((seq, dk), lambda b, i: (b, qb + i)),
                  pl.BlockSpec((seq, dk), lambda b, i: (b, kb + i)),
                  pl.BlockSpec((seq, dv), lambda b, i: (b, vb + i)),
                  pl.BlockSpec((seq, dv), lambda b, i: (b, zb + i)),
                  pl.BlockSpec((seq, LANES), lambda b, i: (b, 0)),
                  pl.BlockSpec((2, LANES, dk), lambda b, i: (0, 0, i)),
                  pl.BlockSpec((2, 1, dk), lambda b, i: (0, 0, i)),
                  pl.BlockSpec((1, dv), lambda b, i: (0, 0)),
                  pl.BlockSpec(cmats.shape, lambda b, i: (0, 0, 0))],
        out_specs=pl.BlockSpec((seq, dv), lambda b, i: (b, i)),
        out_shape=jax.ShapeDtypeStruct((m, width), BF16),
        scratch_shapes=[pltpu.VMEM((dv, seq), BF16),
                        pltpu.VMEM((2, dv, dk), F32),
                        pltpu.VMEM((seq, dv), F32)],
        compiler_params=_params("parallel", "parallel"),
        name="gla",
    )(proj, proj, proj, proj, lr, wa, ba, gnorm, cmats)


def _attn_kernel(lam_init, relb_ref, lam_ref, dn_ref, q_ref, k_ref, v_ref, z_ref, o_ref,
                 bias_scr, s_scr, m_scr):
    h = pl.program_id(0)
    s_len, d2 = q_ref.shape
    d = d2 // 2
    tq = bias_scr.shape[0]
    kt = min(ATTN_KEY_TILE, s_len)
    n_kt = s_len // kt
    n_tiles = bias_scr.shape[1] // LANES

    @pl.when(pl.program_id(1) == 0)
    def _():
        def tile(t, carry):
            c0 = pl.multiple_of(t * LANES, LANES)
            rel = (lax.broadcasted_iota(jnp.int32, (tq, LANES), 1) + (c0 - (s_len - tq))
                   - lax.broadcasted_iota(jnp.int32, (tq, LANES), 0))
            dist = jnp.abs(rel)
            neg = jnp.full((tq, LANES), relb_ref[0, h], F32)
            pos = jnp.full((tq, LANES), relb_ref[REL_BUCKETS // 2, h], F32)
            for bucket, start in enumerate(T5_STARTS, start=1):
                far = dist >= start
                neg = jnp.where(far, relb_ref[bucket, h], neg)
                pos = jnp.where(far, relb_ref[REL_BUCKETS // 2 + bucket, h], pos)
            bias_scr[:, pl.ds(c0, LANES)] = jnp.where(rel > 0, pos, neg) * LOG2E
            return carry
        lax.fori_loop(0, n_tiles, tile, 0)

    lv = lam_ref[...]
    lam = (jnp.exp(jnp.sum(lv[0:1] * lv[1:2], axis=-1, keepdims=True))
           - jnp.exp(jnp.sum(lv[2:3] * lv[3:4], axis=-1, keepdims=True)) + lam_init)

    keys = [slice(j * kt, (j + 1) * kt) for j in range(n_kt)]
    n_blocks = s_len // tq

    def logits_stage(i, slot):
        r0 = pl.multiple_of(i * tq, tq)
        off = s_len - tq - r0
        q = (q_ref[pl.ds(r0, tq), :].astype(F32) * (d ** -0.5 * LOG2E)).astype(BF16)
        for m in range(2):
            mx = jnp.full((tq, LANES), -jnp.inf, F32)
            for j in range(n_kt):
                t = (_dot_nt(q[:, m * d:(m + 1) * d], k_ref[keys[j], m * d:(m + 1) * d])
                     + bias_scr[:, pl.ds(pl.multiple_of(off + j * kt, LANES), kt)])
                s_scr[slot, m, :, keys[j]] = t
                for c in range(kt // LANES):
                    mx = jnp.maximum(mx, t[:, c * LANES:(c + 1) * LANES])
            m_scr[slot, m] = mx

    def softmax_stage(i, slot):
        rows = pl.ds(pl.multiple_of(i * tq, tq), tq)
        den = []
        for m in range(2):
            mx = jnp.max(m_scr[slot, m], axis=-1, keepdims=True)
            acc = jnp.zeros((tq, LANES), F32)
            for j in range(n_kt):
                p = jnp.exp2(s_scr[slot, m, :, keys[j]] - mx)
                s_scr[slot, m, :, keys[j]] = p
                for c in range(kt // LANES):
                    acc = acc + p[:, c * LANES:(c + 1) * LANES]
            den.append(jnp.sum(acc, axis=-1, keepdims=True))
        ratio = lam * den[0] / den[1]
        o = jnp.zeros((tq, d2), F32)
        for j in range(n_kt):
            a = (s_scr[slot, 0, :, keys[j]] - s_scr[slot, 1, :, keys[j]] * ratio).astype(BF16)
            o = o + _dot(a, v_ref[keys[j], :])
        o = o / den[0]
        y = o * lax.rsqrt(jnp.mean(o * o, axis=-1, keepdims=True) + EPS) * dn_ref[...]
        y = y * (1.0 - lam_init) * _silu(z_ref[rows, :].astype(F32))
        o_ref[rows, :] = y.astype(o_ref.dtype)

    logits_stage(0, 0)
    if n_blocks % 2 == 0:
        def pair(p, carry):
            logits_stage(2 * p + 1, 1)
            softmax_stage(2 * p, 0)
            logits_stage(2 * p + 2, 0)
            softmax_stage(2 * p + 1, 1)
            return carry
        lax.fori_loop(0, n_blocks // 2 - 1, pair, 0)
        logits_stage(n_blocks - 1, 1)
        softmax_stage(n_blocks - 2, 0)
        softmax_stage(n_blocks - 1, 1)
    else:
        assert n_blocks == 1
        softmax_stage(0, 0)


def _attn_call(proj, rel_bias, lam_par, dnorm, lam_init, batch, seq, width, col0):
    m = proj.shape[0]
    nh = DIFF_HEADS
    hw = width // nh
    c0 = col0 // hw
    tq = min(ATTN_Q_ROWS, seq)
    blk = lambda c: pl.BlockSpec((seq, hw), lambda h, b: (b, c + h))
    return pl.pallas_call(
        functools.partial(_attn_kernel, lam_init),
        grid=(nh, batch),
        in_specs=[pl.BlockSpec(memory_space=pltpu.SMEM),
                  pl.BlockSpec(lam_par.shape, lambda h, b: (0, 0)),
                  pl.BlockSpec((1, hw), lambda h, b: (0, 0)),
                  blk(c0), blk(c0 + nh), blk(c0 + 2 * nh), blk(c0 + 3 * nh)],
        out_specs=pl.BlockSpec((seq, hw), lambda h, b: (b, h)),
        out_shape=jax.ShapeDtypeStruct((m, width), BF16),
        scratch_shapes=[pltpu.VMEM((tq, 2 * seq - tq), F32),
                        pltpu.VMEM((2, 2, tq, seq), F32),
                        pltpu.VMEM((2, 2, tq, LANES), F32)],
        compiler_params=_params("arbitrary", "arbitrary"),
        name="diff_attn",
    )(rel_bias, lam_par, dnorm, proj, proj, proj, proj)


def _merge_kernel(h_ref, za_ref, zb_ref, zc_ref, wm0_ref, wm1_ref, wm2_ref,
                  bm0_ref, bm1_ref, bm2_ref, wb0_ref, wb1_ref, wb2_ref, o_ref):
    hh = h_ref[...]
    acc = None
    for z_ref, wm_ref, bm_ref, wb_ref in ((za_ref, wm0_ref, bm0_ref, wb0_ref),
                                          (zb_ref, wm1_ref, bm1_ref, wb1_ref),
                                          (zc_ref, wm2_ref, bm2_ref, wb2_ref)):
        gate = _sigmoid(_dot(hh, wm_ref[...]) + bm_ref[...])
        term = gate * _dot(z_ref[...], wb_ref[0])
        acc = term if acc is None else acc + term
    o_ref[...] = acc.astype(o_ref.dtype)


def _merge_call(h, za, zb, zc, wm, bm, wb, layer):
    m, d = h.shape
    tm = min(512, m)
    tn = min(512, d)
    nj = d // tn
    row = pl.BlockSpec((tm, d), lambda j, i: (i, 0))
    wm_spec = lambda br: pl.BlockSpec((None, d, tn), lambda j, i: (layer, 0, br * nj + j))
    bm_spec = lambda br: pl.BlockSpec((None, 1, tn), lambda j, i: (layer, 0, br * nj + j))
    wb_spec = lambda br: pl.BlockSpec((None, 1, d, tn), lambda j, i: (layer, br, 0, j))
    return pl.pallas_call(
        _merge_kernel,
        grid=(nj, m // tm),
        in_specs=[row, row, row, row,
                  wm_spec(0), wm_spec(1), wm_spec(2),
                  bm_spec(0), bm_spec(1), bm_spec(2),
                  wb_spec(0), wb_spec(1), wb_spec(2)],
        out_specs=pl.BlockSpec((tm, tn), lambda j, i: (i, j)),
        out_shape=jax.ShapeDtypeStruct((m, d), BF16),
        compiler_params=_params("arbitrary", "arbitrary"),
        name="merge",
    )(h, za, zb, zc, wm, wm, wm, bm, bm, bm, wb, wb, wb)


def _out_kernel(mg_ref, w_ref, x_ref, g_ref, o_ref):
    out = _dot(mg_ref[...], w_ref[...])
    y = out * lax.rsqrt(jnp.mean(out * out, axis=-1, keepdims=True) + EPS) * g_ref[...]
    o_ref[...] = x_ref[...] + y


def _out_call(merged, w, layer, x2, g):
    m, d = x2.shape
    tm = min(512, m)
    return pl.pallas_call(
        _out_kernel,
        grid=(m // tm,),
        in_specs=[pl.BlockSpec((tm, d), lambda i: (i, 0)),
                  pl.BlockSpec((None, d, d), lambda i: (layer, 0, 0)),
                  pl.BlockSpec((tm, d), lambda i: (i, 0)),
                  pl.BlockSpec((1, d), lambda i: (0, 0))],
        out_specs=pl.BlockSpec((tm, d), lambda i: (i, 0)),
        out_shape=jax.ShapeDtypeStruct((m, d), F32),
        compiler_params=_params("parallel"),
        name="out_proj",
    )(merged, w, x2, g.reshape(1, d))


def kernel(x, norm_pre, w_in, gmlp_ln_g, gmlp_ln_b, gmlp_ws, gmlp_bs, gla_wa2, gla_ba, gla_norm,
           diff_lambda, diff_norm, rel_bias, w_branch, w_merge, b_merge, w_out, norm_post):
    batch, seq, d = x.shape
    depth = norm_pre.shape[0]
    width = d
    m = batch * seq
    hk = gla_wa2.shape[-1]
    lr_col = 3 * width + 2 * hk + 2 * width
    lr_w = 2 * GLA_RANK
    assert seq % GLA_CHUNK == 0 and seq % ATTN_Q_ROWS == 0 and seq % GMLP_CHUNK == 0
    assert w_in.shape[-1] == lr_col + lr_w + 4 * width

    cmats = jnp.asarray(_gla_constants(), BF16)
    w_t = jnp.swapaxes(w_in, 1, 2)
    ws_b = gmlp_ws.astype(BF16)
    wm_b = w_merge.astype(BF16)
    wb_b = w_branch.astype(BF16)
    wo_b = w_out.astype(BF16)
    bm = b_merge.reshape(depth, 1, -1)

    x2 = x.reshape(m, d)
    for l in range(depth):
        wa = jnp.zeros((2, LANES, hk), F32)
        wa = wa.at[0, :GLA_RANK].set(gla_wa2[l, 0]).at[1, GLA_RANK:lr_w].set(gla_wa2[l, 1]).astype(BF16)
        ba = gla_ba[l].reshape(2, 1, hk)

        h = _rms_call(x2, norm_pre[l])
        proj_ab = _mm_call(h, w_t, l, 0, lr_col, "in_proj_ab")
        proj_c = _mm_call(h, w_t, l, lr_col + lr_w, 4 * width, "in_proj_c")
        lr = _mm_call(h, w_t, l, lr_col, LANES, "lr_proj")

        za = _gmlp_call(proj_ab, gmlp_ln_g[l].reshape(1, width), gmlp_ln_b[l].reshape(1, width),
                        ws_b, gmlp_bs[l].T, l, width)
        zb = _gla_call(proj_ab, lr, wa, ba, gla_norm[l].reshape(1, -1), cmats, batch, seq, width,
                       3 * width)
        lam_init = 0.8 - 0.6 * math.exp(-0.3 * l)
        zc = _attn_call(proj_c, rel_bias, diff_lambda[l], diff_norm[l].reshape(1, -1), lam_init,
                        batch, seq, width, 0)
        merged = _merge_call(h, za, zb, zc, wm_b, bm, wb_b, l)
        x2 = _out_call(merged, wo_b, l, x2, norm_post[l])
    return x2.reshape(batch, seq, d)
```

```python
import functools
import math

import numpy as np
import jax
import jax.numpy as jnp
from jax import lax
from jax.experimental import pallas as pl
from jax.experimental.pallas import tpu as pltpu

F32 = jnp.float32
BF16 = jnp.bfloat16

EPS = 1e-6
LOG2E = math.log2(math.e)
N_BRANCH = 3
GMLP_CHUNK = 128
GMLP_GROUPS = 8
GLA_HEADS = 4
GLA_RANK = 16
GLA_TAU = 16.0
GLA_SUB = 64
GLA_CHUNK = 2 * GLA_SUB
GLA_GROUP = 4
DIFF_HEADS = 8
ATTN_Q_ROWS = 256
ATTN_BIAS_ROWS = 32
REL_BUCKETS = 32
T5_STARTS = (1, 2, 3, 4, 5, 6, 7, 8, 12, 16, 23, 32, 46, 64, 91)

LANES = 128
SUBLANES = 8
BF16_ROWS = 2 * SUBLANES
VMEM_LIMIT = 52 * 1024 * 1024


def _params(*sem):
    return pltpu.CompilerParams(dimension_semantics=sem, vmem_limit_bytes=VMEM_LIMIT)


def _gelu(x):
    return 0.5 * x * (1.0 + jnp.tanh(math.sqrt(2.0 / math.pi) * (x + 0.044715 * (x * x * x))))


def _sigmoid(x):
    return 1.0 / (1.0 + jnp.exp(-x))


def _silu(x):
    return x * _sigmoid(x)


def _log_sigmoid(x):
    return jnp.minimum(x, 0.0) - jnp.log1p(jnp.exp(-jnp.abs(x)))


def _dot(a, b):
    return jnp.dot(a, b, preferred_element_type=F32)


def _dot_nt(a, b):
    return lax.dot_general(a, b, (((1,), (1,)), ((), ())), preferred_element_type=F32)


def _rms_kernel(x_ref, g_ref, o_ref):
    x = x_ref[...]
    y = x * lax.rsqrt(jnp.mean(x * x, axis=-1, keepdims=True) + EPS)
    o_ref[...] = (y * g_ref[...]).astype(o_ref.dtype)


def _rms_call(x2, g):
    m, d = x2.shape
    tm = min(512, m)
    return pl.pallas_call(
        _rms_kernel,
        grid=(m // tm,),
        in_specs=[pl.BlockSpec((tm, d), lambda i: (i, 0)),
                  pl.BlockSpec((1, d), lambda i: (0, 0))],
        out_specs=pl.BlockSpec((tm, d), lambda i: (i, 0)),
        out_shape=jax.ShapeDtypeStruct((m, d), BF16),
        compiler_params=_params("parallel"),
        name="rms_pre",
    )(x2, g.reshape(1, d))


def _mm_kernel(a_ref, w_ref, o_ref, w_scr):
    @pl.when(pl.program_id(1) == 0)
    def _():
        w_scr[...] = w_ref[0].astype(BF16)
    o_ref[...] = _dot_nt(a_ref[...], w_scr[...]).astype(o_ref.dtype)


def _mm_call(a, w_t, layer, row0, n, name):
    m, k = a.shape
    tm = min(1024, m)
    tn = min(1024, n)
    return pl.pallas_call(
        _mm_kernel,
        grid=(n // tn, m // tm),
        in_specs=[pl.BlockSpec((tm, k), lambda j, i: (i, 0)),
                  pl.BlockSpec((pl.Element(1), pl.Element(tn), pl.Element(k)),
                               lambda j, i: (layer, pl.multiple_of(row0 + j * tn, SUBLANES), 0))],
        out_specs=pl.BlockSpec((tm, tn), lambda j, i: (i, j)),
        out_shape=jax.ShapeDtypeStruct((m, n), BF16),
        scratch_shapes=[pltpu.VMEM((tn, k), BF16)],
        compiler_params=_params("arbitrary", "arbitrary"),
        name=name,
    )(a, w_t)


def _gmlp_kernel(u_ref, v_ref, z_ref, lng_ref, lnb_ref, ws_ref, bs_ref, o_ref):
    tm, w = v_ref.shape
    gc = w // GMLP_GROUPS
    gv = _gelu(v_ref[...].astype(F32))
    mu = jnp.mean(gv, axis=-1, keepdims=True)
    cen = gv - mu
    var = jnp.mean(cen * cen, axis=-1, keepdims=True)
    sv = (cen * lax.rsqrt(var + EPS) * lng_ref[...] + lnb_ref[...]).astype(BF16)
    for c in range(tm // GMLP_CHUNK):
        rows = slice(c * GMLP_CHUNK, (c + 1) * GMLP_CHUNK)
        for g in range(GMLP_GROUPS):
            cols = slice(g * gc, (g + 1) * gc)
            mixed = _dot(ws_ref[g], sv[rows, cols]) + bs_ref[:, g:g + 1]
            u = _gelu(u_ref[rows, cols].astype(F32))
            z = _silu(z_ref[rows, cols].astype(F32))
            o_ref[rows, cols] = (u * mixed * z).astype(o_ref.dtype)


def _gmlp_call(proj, lng, lnb, ws, bs_t, layer, width):
    m = proj.shape[0]
    tm = min(256, m)
    blk = lambda c: pl.BlockSpec((tm, width), lambda i: (i, c))
    full = lambda a: pl.BlockSpec(a.shape, lambda i: (0,) * a.ndim)
    return pl.pallas_call(
        _gmlp_kernel,
        grid=(m // tm,),
        in_specs=[blk(0), blk(1), blk(2), full(lng), full(lnb),
                  pl.BlockSpec((None,) + ws.shape[1:], lambda i: (layer, 0, 0, 0)), full(bs_t)],
        out_specs=pl.BlockSpec((tm, width), lambda i: (i, 0)),
        out_shape=jax.ShapeDtypeStruct((m, width), BF16),
        compiler_params=_params("parallel"),
        name="gmlp",
    )(proj, proj, proj, lng, lnb, ws, bs_t)


def _gla_constants():
    c, s = GLA_CHUNK, GLA_SUB
    i = np.arange(c)[:, None]
    j = np.arange(c)[None, :]
    same = (i // s) == (j // s)
    mats = []
    for rev in (False, True):
        if not rev:
            cum = same & (j <= i)
            ref = same & ((j % s) <= s // 2 - 1)
        else:
            cum = same & (j >= i)
            ref = same & ((j % s) >= s // 2)
        cum, ref, tot = (t.astype(np.float32) for t in (cum, ref, same))
        blocks = np.concatenate([cum, cum - ref, ref - cum, tot - cum], axis=0)
        mats.append(np.concatenate([blocks, blocks], axis=1))
    return np.stack(mats)


def _gla_kernel(q_ref, k_ref, v_ref, z_ref, lr_ref, wa_ref, ba_ref, gn_ref, cm_ref, o_ref,
                vt_scr, st_scr, o_scr):
    s_len, dk = q_ref.shape
    c, sub = GLA_CHUNK, GLA_SUB
    n_chunks = s_len // c

    vt_scr[...] = v_ref[...].astype(F32).T.astype(BF16)
    st_scr[...] = jnp.zeros_like(st_scr)

    ri = lax.broadcasted_iota(jnp.int32, (c, c), 0)
    ci = lax.broadcasted_iota(jnp.int32, (c, c), 1)
    same = (ri // sub) == (ci // sub)
    row_in_sub0 = lax.broadcasted_iota(jnp.int32, (c, 1), 0) < sub

    n_grp = min(GLA_GROUP, n_chunks)
    assert n_chunks % n_grp == 0

    def group(d, gi, first):
        rev = d == 1
        cs = range(n_grp)
        base = pl.multiple_of(gi * (n_grp * c), n_grp * c)
        rows = pl.ds(base, n_grp * c)
        crow = [pl.ds(pl.multiple_of(base + i * c, c), c) for i in cs]
        part = lambda x, i: x[i * c:(i + 1) * c]
        q = q_ref[rows, :].astype(F32) * (dk ** -0.5)
        k = k_ref[rows, :].astype(F32)
        g = _log_sigmoid(_dot(lr_ref[rows, :], wa_ref[d]) + ba_ref[d]) * (LOG2E / GLA_TAU)
        g_hi = g.astype(BF16)
        g_lo = (g - g_hi.astype(F32)).astype(BF16)
        e = [_dot(cm_ref[d], jnp.concatenate([part(g_hi, i), part(g_lo, i)], axis=0)) for i in cs]
        cum = [e[i][:c] for i in cs]
        if rev:
            last0, last1 = [x[0:1] for x in cum], [x[sub:sub + 1] for x in cum]
            diag_mask = same & (ci >= ri)
            off_mask = (ri < sub) & (ci >= sub)
        else:
            last0, last1 = [x[sub - 1:sub] for x in cum], [x[c - 1:c] for x in cum]
            diag_mask = same & (ci <= ri)
            off_mask = (ri >= sub) & (ci < sub)
        q_in = [(part(q, i) * jnp.exp2(e[i][c:2 * c])).astype(BF16) for i in cs]
        k_in = [(part(k, i) * jnp.exp2(e[i][2 * c:3 * c])).astype(BF16) for i in cs]
        q_dec = [part(q, i) * jnp.exp2(cum[i]) for i in cs]
        k_dec = [part(k, i) * jnp.exp2(e[i][3 * c:]) for i in cs]
        s_diag = [_dot_nt(q_in[i], k_in[i]) for i in cs]
        s_off = [_dot_nt(q_dec[i].astype(BF16), k_dec[i].astype(BF16)) for i in cs]
        scores = [(jnp.where(diag_mask, s_diag[i], 0.0)
                   + jnp.where(off_mask, s_off[i], 0.0)).astype(BF16) for i in cs]
        o_intra = [_dot(scores[i], v_ref[crow[i], :]) for i in cs]
        if rev:
            q_st = [q_dec[i] * jnp.where(row_in_sub0, jnp.exp2(last1[i]), 1.0) for i in cs]
            k_st = [k_dec[i] * jnp.where(row_in_sub0, 1.0, jnp.exp2(last0[i])) for i in cs]
        else:
            q_st = [q_dec[i] * jnp.where(row_in_sub0, 1.0, jnp.exp2(last0[i])) for i in cs]
            k_st = [k_dec[i] * jnp.where(row_in_sub0, jnp.exp2(last1[i]), 1.0) for i in cs]
        kv_t = [_dot(vt_scr[:, crow[i]], k_st[i].astype(BF16)) for i in cs]
        st = st_scr[d]
        seen = [None] * n_grp
        for i in (reversed(cs) if rev else cs):
            seen[i] = st.astype(BF16)
            st = st * jnp.exp2(last0[i] + last1[i]) + kv_t[i]
        st_scr[d] = st
        for i in cs:
            o = o_intra[i] + _dot_nt(q_st[i].astype(BF16), seen[i])
            if first:
                o_scr[crow[i], :] = o
            else:
                o_scr[crow[i], :] += o

    n_groups = n_chunks // n_grp
    half = n_groups // 2

    def sweep(first):
        def body(t, carry):
            group(0, t, first)
            group(1, n_groups - 1 - t, first)
            return carry
        return body

    if n_groups % 2 == 0:
        lax.fori_loop(0, half, sweep(True), 0)
        lax.fori_loop(half, n_groups, sweep(False), 0)
    else:
        o_scr[...] = jnp.zeros_like(o_scr)
        lax.fori_loop(0, n_groups, sweep(False), 0)

    o = o_scr[...]
    y = o * lax.rsqrt(jnp.mean(o * o, axis=-1, keepdims=True) + EPS) * gn_ref[...]
    o_ref[...] = (y * _silu(z_ref[...].astype(F32))).astype(o_ref.dtype)


def _gla_call(proj, lr, wa, ba, gnorm, cmats, batch, seq, width, col0):
    m = proj.shape[0]
    h = GLA_HEADS
    dv = width // h
    dk = dv // 2
    qb, kb = col0 // dk, col0 // dk + h
    vb, zb = (col0 + 2 * h * dk) // dv, (col0 + 2 * h * dk) // dv + h
    return pl.pallas_call(
        _gla_kernel,
        grid=(batch, h),
        in_specs=[pl.BlockSpec((seq, dk), lambda b, i: (b, qb + i)),
                  pl.BlockSpec((seq, dk), lambda b, i: (b, kb + i)),
                  pl.BlockSpec((seq, dv), lambda b, i: (b, vb + i)),
                  pl.BlockSpec((seq, dv), lambda b, i: (b, zb + i)),
                  pl.BlockSpec((seq, LANES), lambda b, i: (b, 0)),
                  pl.BlockSpec((2, LANES, dk), lambda b, i: (0, 0, i)),
                  pl.BlockSpec((2, 1, dk), lambda b, i: (0, 0, i)),
                  pl.BlockSpec((1, dv), lambda b, i: (0, 0)),
                  pl.BlockSpec(cmats.shape, lambda b, i: (0, 0, 0))],
        out_specs=pl.BlockSpec((seq, dv), lambda b, i: (b, i)),
        out_shape=jax.ShapeDtypeStruct((m, width), BF16),
        scratch_shapes=[pltpu.VMEM((dv, seq), BF16),
                        pltpu.VMEM((2, dv, dk), F32),
                        pltpu.VMEM((seq, dv), F32)],
        compiler_params=_params("parallel", "parallel"),
        name="gla",
    )(proj, proj, proj, proj, lr, wa, ba, gnorm, cmats)


def _attn_kernel(lam_init, relb_ref, lam_ref, dn_ref, q_ref, k_ref, v_ref, z_ref, o_ref,
                 bias_scr, s_scr, m_scr, vt_scr):
    h = pl.program_id(0)
    s_len, d2 = q_ref.shape
    d = d2 // 2
    tq = bias_scr.shape[1]
    bt = ATTN_BIAS_ROWS
    n_tiles = bias_scr.shape[0] // bt

    @pl.when(pl.program_id(1) == 0)
    def _():
        def tile(t, carry):
            c0 = pl.multiple_of(t * bt, bt)
            rel = (lax.broadcasted_iota(jnp.int32, (bt, tq), 0) + (c0 - (s_len - tq))
                   - lax.broadcasted_iota(jnp.int32, (bt, tq), 1))
            dist = jnp.abs(rel)
            neg = jnp.full((bt, tq), relb_ref[0, h], F32)
            pos = jnp.full((bt, tq), relb_ref[REL_BUCKETS // 2, h], F32)
            for bucket, start in enumerate(T5_STARTS, start=1):
                far = dist >= start
                neg = jnp.where(far, relb_ref[bucket, h], neg)
                pos = jnp.where(far, relb_ref[REL_BUCKETS // 2 + bucket, h], pos)
            bias_scr[pl.ds(c0, bt), :] = jnp.where(rel > 0, pos, neg) * LOG2E
            return carry
        lax.fori_loop(0, n_tiles, tile, 0)

    vt_scr[:d2, :] = v_ref[...].astype(F32).T.astype(BF16)
    vt_scr[d2:, :] = jnp.ones((vt_scr.shape[0] - d2, s_len), BF16)

    lv = lam_ref[...]
    lam = (jnp.exp(jnp.sum(lv[0:1] * lv[1:2], axis=-1, keepdims=True))
           - jnp.exp(jnp.sum(lv[2:3] * lv[3:4], axis=-1, keepdims=True)) + lam_init)

    n_blocks = s_len // tq

    def logits_stage(i, slot):
        r0 = pl.multiple_of(i * tq, tq)
        bias = bias_scr[pl.ds(pl.multiple_of(s_len - tq - r0, tq), s_len), :]
        q = (q_ref[pl.ds(r0, tq), :].astype(F32) * (d ** -0.5 * LOG2E)).astype(BF16)
        for m in range(2):
            t = _dot_nt(k_ref[:, m * d:(m + 1) * d], q[:, m * d:(m + 1) * d]) + bias
            s_scr[slot, m] = t
            m_scr[slot, m] = jnp.max(t, axis=0, keepdims=True)

    def softmax_stage(i, slot):
        rows = pl.ds(pl.multiple_of(i * tq, tq), tq)
        num, den = [], []
        for m in range(2):
            p = jnp.exp2(s_scr[slot, m] - m_scr[slot, m]).astype(BF16)
            r = _dot(vt_scr[...], p)
            num.append(r[:d2])
            den.append(r[d2:d2 + 1])
        o = (num[0] / den[0] - num[1] * (lam / den[1])).T
        y = o * lax.rsqrt(jnp.mean(o * o, axis=-1, keepdims=True) + EPS) * dn_ref[...]
        y = y * (1.0 - lam_init) * _silu(z_ref[rows, :].astype(F32))
        o_ref[rows, :] = y.astype(o_ref.dtype)

    logits_stage(0, 0)
    if n_blocks % 2 == 0:
        def pair(p, carry):
            logits_stage(2 * p + 1, 1)
            softmax_stage(2 * p, 0)
            logits_stage(2 * p + 2, 0)
            softmax_stage(2 * p + 1, 1)
            return carry
        lax.fori_loop(0, n_blocks // 2 - 1, pair, 0)
        logits_stage(n_blocks - 1, 1)
        softmax_stage(n_blocks - 2, 0)
        softmax_stage(n_blocks - 1, 1)
    else:
        assert n_blocks == 1
        softmax_stage(0, 0)


def _attn_call(proj, rel_bias, lam_par, dnorm, lam_init, batch, seq, width, col0):
    m = proj.shape[0]
    nh = DIFF_HEADS
    hw = width // nh
    c0 = col0 // hw
    tq = min(ATTN_Q_ROWS, seq)
    blk = lambda c: pl.BlockSpec((seq, hw), lambda h, b: (b, c + h))
    return pl.pallas_call(
        functools.partial(_attn_kernel, lam_init),
        grid=(nh, batch),
        in_specs=[pl.BlockSpec(memory_space=pltpu.SMEM),
                  pl.BlockSpec(lam_par.shape, lambda h, b: (0, 0)),
                  pl.BlockSpec((1, hw), lambda h, b: (0, 0)),
                  blk(c0), blk(c0 + nh), blk(c0 + 2 * nh), blk(c0 + 3 * nh)],
        out_specs=pl.BlockSpec((seq, hw), lambda h, b: (b, h)),
        out_shape=jax.ShapeDtypeStruct((m, width), BF16),
        scratch_shapes=[pltpu.VMEM((2 * seq - tq, tq), F32),
                        pltpu.VMEM((2, 2, seq, tq), F32),
                        pltpu.VMEM((2, 2, 1, tq), F32),
                        pltpu.VMEM((hw + BF16_ROWS, seq), BF16)],
        compiler_params=_params("arbitrary", "arbitrary"),
        name="diff_attn",
    )(rel_bias, lam_par, dnorm, proj, proj, proj, proj)


def _merge_kernel(h_ref, za_ref, zb_ref, zc_ref, wm0_ref, wm1_ref, wm2_ref,
                  bm0_ref, bm1_ref, bm2_ref, wb0_ref, wb1_ref, wb2_ref, o_ref):
    hh = h_ref[...]
    acc = None
    for z_ref, wm_ref, bm_ref, wb_ref in ((za_ref, wm0_ref, bm0_ref, wb0_ref),
                                          (zb_ref, wm1_ref, bm1_ref, wb1_ref),
                                          (zc_ref, wm2_ref, bm2_ref, wb2_ref)):
        gate = _sigmoid(_dot(hh, wm_ref[...]) + bm_ref[...])
        term = gate * _dot(z_ref[...], wb_ref[0])
        acc = term if acc is None else acc + term
    o_ref[...] = acc.astype(o_ref.dtype)


def _merge_call(h, za, zb, zc, wm, bm, wb, layer):
    m, d = h.shape
    tm = min(512, m)
    tn = min(512, d)
    nj = d // tn
    row = pl.BlockSpec((tm, d), lambda j, i: (i, 0))
    wm_spec = lambda br: pl.BlockSpec((None, d, tn), lambda j, i: (layer, 0, br * nj + j))
    bm_spec = lambda br: pl.BlockSpec((None, 1, tn), lambda j, i: (layer, 0, br * nj + j))
    wb_spec = lambda br: pl.BlockSpec((None, 1, d, tn), lambda j, i: (layer, br, 0, j))
    return pl.pallas_call(
        _merge_kernel,
        grid=(nj, m // tm),
        in_specs=[row, row, row, row,
                  wm_spec(0), wm_spec(1), wm_spec(2),
                  bm_spec(0), bm_spec(1), bm_spec(2),
                  wb_spec(0), wb_spec(1), wb_spec(2)],
        out_specs=pl.BlockSpec((tm, tn), lambda j, i: (i, j)),
        out_shape=jax.ShapeDtypeStruct((m, d), BF16),
        compiler_params=_params("arbitrary", "arbitrary"),
        name="merge",
    )(h, za, zb, zc, wm, wm, wm, bm, bm, bm, wb, wb, wb)


def _out_kernel(mg_ref, w_ref, x_ref, g_ref, gn_ref, o_ref, *maybe_h_ref):
    out = _dot(mg_ref[...], w_ref[...])
    y = out * lax.rsqrt(jnp.mean(out * out, axis=-1, keepdims=True) + EPS) * g_ref[...]
    x = x_ref[...] + y
    o_ref[...] = x
    if maybe_h_ref:
        hn = x * lax.rsqrt(jnp.mean(x * x, axis=-1, keepdims=True) + EPS)
        maybe_h_ref[0][...] = (hn * gn_ref[...]).astype(BF16)


def _out_call(merged, w, layer, x2, g, g_next):
    m, d = x2.shape
    tm = min(512, m)
    row = pl.BlockSpec((tm, d), lambda i: (i, 0))
    vec = pl.BlockSpec((1, d), lambda i: (0, 0))
    with_h = g_next is not None
    out = pl.pallas_call(
        _out_kernel,
        grid=(m // tm,),
        in_specs=[row, pl.BlockSpec((None, d, d), lambda i: (layer, 0, 0)), row, vec, vec],
        out_specs=[row, row] if with_h else row,
        out_shape=([jax.ShapeDtypeStruct((m, d), F32), jax.ShapeDtypeStruct((m, d), BF16)]
                   if with_h else jax.ShapeDtypeStruct((m, d), F32)),
        compiler_params=_params("parallel"),
        name="out_proj",
    )(merged, w, x2, g.reshape(1, d), (g_next if with_h else g).reshape(1, d))
    return out if with_h else (out, None)


def kernel(x, norm_pre, w_in, gmlp_ln_g, gmlp_ln_b, gmlp_ws, gmlp_bs, gla_wa2, gla_ba, gla_norm,
           diff_lambda, diff_norm, rel_bias, w_branch, w_merge, b_merge, w_out, norm_post):
    batch, seq, d = x.shape
    depth = norm_pre.shape[0]
    width = d
    m = batch * seq
    hk = gla_wa2.shape[-1]
    lr_col = 3 * width + 2 * hk + 2 * width
    lr_w = 2 * GLA_RANK
    assert seq % GLA_CHUNK == 0 and seq % ATTN_Q_ROWS == 0 and seq % GMLP_CHUNK == 0
    assert w_in.shape[-1] == lr_col + lr_w + 4 * width

    cmats = jnp.asarray(_gla_constants(), BF16)
    w_t = jnp.swapaxes(w_in, 1, 2)
    ws_b = gmlp_ws.astype(BF16)
    wm_b = w_merge.astype(BF16)
    wb_b = w_branch.astype(BF16)
    wo_b = w_out.astype(BF16)
    bm = b_merge.reshape(depth, 1, -1)

    x2 = x.reshape(m, d)
    h = _rms_call(x2, norm_pre[0])
    for l in range(depth):
        wa = jnp.zeros((2, LANES, hk), F32)
        wa = wa.at[0, :GLA_RANK].set(gla_wa2[l, 0]).at[1, GLA_RANK:lr_w].set(gla_wa2[l, 1]).astype(BF16)
        ba = gla_ba[l].reshape(2, 1, hk)

        proj_ab = _mm_call(h, w_t, l, 0, lr_col, "in_proj_ab")
        proj_c = _mm_call(h, w_t, l, lr_col + lr_w, 4 * width, "in_proj_c")
        lr = _mm_call(h, w_t, l, lr_col, LANES, "lr_proj")

        za = _gmlp_call(proj_ab, gmlp_ln_g[l].reshape(1, width), gmlp_ln_b[l].reshape(1, width),
                        ws_b, gmlp_bs[l].T, l, width)
        zb = _gla_call(proj_ab, lr, wa, ba, gla_norm[l].reshape(1, -1), cmats, batch, seq, width,
                       3 * width)
        lam_init = 0.8 - 0.6 * math.exp(-0.3 * l)
        zc = _attn_call(proj_c, rel_bias, diff_lambda[l], diff_norm[l].reshape(1, -1), lam_init,
                        batch, seq, width, 0)
        merged = _merge_call(h, za, zb, zc, wm_b, bm, wb_b, l)
        x2, h = _out_call(merged, wo_b, l, x2, norm_post[l],
                          norm_pre[l + 1] if l + 1 < depth else None)
    return x2.reshape(batch, seq, d)
```

```python
import functools
import math

import numpy as np
import jax
import jax.numpy as jnp
from jax import lax
from jax.experimental import pallas as pl
from jax.experimental.pallas import tpu as pltpu

F32 = jnp.float32
BF16 = jnp.bfloat16

EPS = 1e-6
LOG2E = math.log2(math.e)
N_BRANCH = 3
GMLP_CHUNK = 128
GMLP_GROUPS = 8
GLA_HEADS = 4
GLA_RANK = 16
GLA_TAU = 16.0
GLA_SUB = 64
GLA_CHUNK = 2 * GLA_SUB
GLA_GROUP = 8
DIFF_HEADS = 8
ATTN_Q_ROWS = 256
ATTN_BIAS_ROWS = 32
REL_BUCKETS = 32
T5_STARTS = (1, 2, 3, 4, 5, 6, 7, 8, 12, 16, 23, 32, 46, 64, 91)

LANES = 128
SUBLANES = 8
BF16_ROWS = 2 * SUBLANES
VMEM_LIMIT = 52 * 1024 * 1024


def _params(*sem):
    return pltpu.CompilerParams(dimension_semantics=sem, vmem_limit_bytes=VMEM_LIMIT)


GELU_C = math.sqrt(2.0 / math.pi)


def _gelu(x):
    hx = 0.5 * x
    return hx + hx * jnp.tanh(x * (GELU_C + (GELU_C * 0.044715) * (x * x)))


def _sigmoid(x):
    return 0.5 + 0.5 * jnp.tanh(0.5 * x)


def _silu(x):
    hx = 0.5 * x
    return hx + hx * jnp.tanh(hx)


def _log2_sigmoid(x):
    u = x * LOG2E
    return jnp.minimum(u, 0.0) - jnp.log2(1.0 + jnp.exp2(-jnp.abs(u)))


def _dot(a, b):
    return jnp.dot(a, b, preferred_element_type=F32)


def _dot_nt(a, b):
    return lax.dot_general(a, b, (((1,), (1,)), ((), ())), preferred_element_type=F32)


def _rms_kernel(x_ref, g_ref, o_ref):
    x = x_ref[...]
    y = x * lax.rsqrt(jnp.mean(x * x, axis=-1, keepdims=True) + EPS)
    o_ref[...] = (y * g_ref[...]).astype(o_ref.dtype)


def _rms_call(x2, g):
    m, d = x2.shape
    tm = min(512, m)
    return pl.pallas_call(
        _rms_kernel,
        grid=(m // tm,),
        in_specs=[pl.BlockSpec((tm, d), lambda i: (i, 0)),
                  pl.BlockSpec((1, d), lambda i: (0, 0))],
        out_specs=pl.BlockSpec((tm, d), lambda i: (i, 0)),
        out_shape=jax.ShapeDtypeStruct((m, d), BF16),
        compiler_params=_params("parallel"),
        name="rms_pre",
    )(x2, g.reshape(1, d))


def _mm_kernel(a_ref, w_ref, o_ref, w_scr):
    @pl.when(pl.program_id(1) == 0)
    def _():
        w_scr[...] = w_ref[0].astype(BF16)
    o_ref[...] = _dot_nt(a_ref[...], w_scr[...]).astype(o_ref.dtype)


def _mm_call(a, w_t, layer, row0, n, name):
    m, k = a.shape
    tm = min(1024, m)
    tn = min(1024, n)
    return pl.pallas_call(
        _mm_kernel,
        grid=(n // tn, m // tm),
        in_specs=[pl.BlockSpec((tm, k), lambda j, i: (i, 0)),
                  pl.BlockSpec((pl.Element(1), pl.Element(tn), pl.Element(k)),
                               lambda j, i: (layer, pl.multiple_of(row0 + j * tn, SUBLANES), 0))],
        out_specs=pl.BlockSpec((tm, tn), lambda j, i: (i, j)),
        out_shape=jax.ShapeDtypeStruct((m, n), BF16),
        scratch_shapes=[pltpu.VMEM((tn, k), BF16)],
        compiler_params=_params("arbitrary", "arbitrary"),
        name=name,
    )(a, w_t)


def _gmlp_kernel(u_ref, v_ref, z_ref, lng_ref, lnb_ref, ws_ref, bs_ref, o_ref):
    tm, w = v_ref.shape
    gc = w // GMLP_GROUPS
    gv = _gelu(v_ref[...].astype(F32))
    mu = jnp.mean(gv, axis=-1, keepdims=True)
    cen = gv - mu
    var = jnp.mean(cen * cen, axis=-1, keepdims=True)
    sv = (cen * lax.rsqrt(var + EPS) * lng_ref[...] + lnb_ref[...]).astype(BF16)
    for c in range(tm // GMLP_CHUNK):
        rows = slice(c * GMLP_CHUNK, (c + 1) * GMLP_CHUNK)
        for g in range(GMLP_GROUPS):
            cols = slice(g * gc, (g + 1) * gc)
            mixed = _dot(ws_ref[g], sv[rows, cols]) + bs_ref[:, g:g + 1]
            u = _gelu(u_ref[rows, cols].astype(F32))
            z = _silu(z_ref[rows, cols].astype(F32))
            o_ref[rows, cols] = (u * mixed * z).astype(o_ref.dtype)


def _gmlp_call(proj, lng, lnb, ws, bs_t, layer, width):
    m = proj.shape[0]
    tm = min(256, m)
    blk = lambda c: pl.BlockSpec((tm, width), lambda i: (i, c))
    full = lambda a: pl.BlockSpec(a.shape, lambda i: (0,) * a.ndim)
    return pl.pallas_call(
        _gmlp_kernel,
        grid=(m // tm,),
        in_specs=[blk(0), blk(1), blk(2), full(lng), full(lnb),
                  pl.BlockSpec((None,) + ws.shape[1:], lambda i: (layer, 0, 0, 0)), full(bs_t)],
        out_specs=pl.BlockSpec((tm, width), lambda i: (i, 0)),
        out_shape=jax.ShapeDtypeStruct((m, width), BF16),
        compiler_params=_params("parallel"),
        name="gmlp",
    )(proj, proj, proj, lng, lnb, ws, bs_t)


def _gla_constants():
    c, s = GLA_CHUNK, GLA_SUB
    i = np.arange(c)[:, None]
    j = np.arange(c)[None, :]
    same = (i // s) == (j // s)
    mats = []
    for rev in (False, True):
        if not rev:
            cum = same & (j <= i)
            ref = same & ((j % s) <= s // 2 - 1)
        else:
            cum = same & (j >= i)
            ref = same & ((j % s) >= s // 2)
        cum, ref, tot = (t.astype(np.float32) for t in (cum, ref, same))
        blocks = np.concatenate([cum, cum - ref, ref - cum, tot - cum], axis=0)
        mats.append(np.concatenate([blocks, blocks], axis=1))
    return np.stack(mats)


def _gla_kernel(q_ref, k_ref, v_ref, z_ref, lr_ref, wa_ref, ba_ref, gn_ref, cm_ref, o_ref,
                vt_scr, st_scr, o_scr):
    s_len, dk = q_ref.shape
    c, sub = GLA_CHUNK, GLA_SUB
    n_chunks = s_len // c

    vt_scr[...] = v_ref[...].astype(F32).T.astype(BF16)
    st_scr[...] = jnp.zeros_like(st_scr)

    ri = lax.broadcasted_iota(jnp.int32, (c, c), 0)
    ci = lax.broadcasted_iota(jnp.int32, (c, c), 1)
    same = (ri // sub) == (ci // sub)
    row_in_sub0 = lax.broadcasted_iota(jnp.int32, (c, 1), 0) < sub

    n_grp = min(GLA_GROUP, n_chunks)
    assert n_chunks % n_grp == 0

    def group(d, gi, first):
        rev = d == 1
        cs = range(n_grp)
        base = pl.multiple_of(gi * (n_grp * c), n_grp * c)
        rows = pl.ds(base, n_grp * c)
        crow = [pl.ds(pl.multiple_of(base + i * c, c), c) for i in cs]
        part = lambda x, i: x[i * c:(i + 1) * c]
        q = q_ref[rows, :].astype(F32) * (dk ** -0.5)
        k = k_ref[rows, :].astype(F32)
        g = _log2_sigmoid(_dot(lr_ref[rows, :], wa_ref[d]) + ba_ref[d]) * (1.0 / GLA_TAU)
        g_hi = g.astype(BF16)
        g_lo = (g - g_hi.astype(F32)).astype(BF16)
        e = [_dot(cm_ref[d], jnp.concatenate([part(g_hi, i), part(g_lo, i)], axis=0)) for i in cs]
        cum = [e[i][:c] for i in cs]
        if rev:
            last0, last1 = [x[0:1] for x in cum], [x[sub:sub + 1] for x in cum]
            diag_mask = same & (ci >= ri)
            off_mask = (ri < sub) & (ci >= sub)
        else:
            last0, last1 = [x[sub - 1:sub] for x in cum], [x[c - 1:c] for x in cum]
            diag_mask = same & (ci <= ri)
            off_mask = (ri >= sub) & (ci < sub)
        q_in = [(part(q, i) * jnp.exp2(e[i][c:2 * c])).astype(BF16) for i in cs]
        k_in = [(part(k, i) * jnp.exp2(e[i][2 * c:3 * c])).astype(BF16) for i in cs]
        q_dec = [part(q, i) * jnp.exp2(cum[i]) for i in cs]
        k_dec = [part(k, i) * jnp.exp2(e[i][3 * c:]) for i in cs]
        s_diag = [_dot_nt(q_in[i], k_in[i]) for i in cs]
        s_off = [_dot_nt(q_dec[i].astype(BF16), k_dec[i].astype(BF16)) for i in cs]
        scores = [(jnp.where(diag_mask, s_diag[i], 0.0)
                   + jnp.where(off_mask, s_off[i], 0.0)).astype(BF16) for i in cs]
        o_intra = [_dot(scores[i], v_ref[crow[i], :]) for i in cs]
        if rev:
            q_st = [q_dec[i] * jnp.where(row_in_sub0, jnp.exp2(last1[i]), 1.0) for i in cs]
            k_st = [k_dec[i] * jnp.where(row_in_sub0, 1.0, jnp.exp2(last0[i])) for i in cs]
        else:
            q_st = [q_dec[i] * jnp.where(row_in_sub0, 1.0, jnp.exp2(last0[i])) for i in cs]
            k_st = [k_dec[i] * jnp.where(row_in_sub0, jnp.exp2(last1[i]), 1.0) for i in cs]
        kv_t = [_dot(vt_scr[:, crow[i]], k_st[i].astype(BF16)) for i in cs]
        st = st_scr[d]
        seen = [None] * n_grp
        for i in (reversed(cs) if rev else cs):
            seen[i] = st.astype(BF16)
            st = st * jnp.exp2(last0[i] + last1[i]) + kv_t[i]
        st_scr[d] = st
        for i in cs:
            o = o_intra[i] + _dot_nt(q_st[i].astype(BF16), seen[i])
            if first:
                o_scr[crow[i], :] = o
            else:
                o_scr[crow[i], :] += o

    n_groups = n_chunks // n_grp
    half = n_groups // 2

    def sweep(first):
        def body(t, carry):
            group(0, t, first)
            group(1, n_groups - 1 - t, first)
            return carry
        return body

    if n_groups == 1:
        group(0, 0, True)
        group(1, 0, False)
    elif n_groups % 2 == 0:
        lax.fori_loop(0, half, sweep(True), 0)
        lax.fori_loop(half, n_groups, sweep(False), 0)
    else:
        o_scr[...] = jnp.zeros_like(o_scr)
        lax.fori_loop(0, n_groups, sweep(False), 0)

    o = o_scr[...]
    y = o * lax.rsqrt(jnp.mean(o * o, axis=-1, keepdims=True) + EPS) * gn_ref[...]
    o_ref[...] = (y * _silu(z_ref[...].astype(F32))).astype(o_ref.dtype)


def _gla_call(proj, lr, wa, ba, gnorm, cmats, batch, seq, width, col0):
    m = proj.shape[0]
    h = GLA_HEADS
    dv = width // h
    dk = dv // 2
    qb, kb = col0 // dk, col0 // dk + h
    vb, zb = (col0 + 2 * h * dk) // dv, (col0 + 2 * h * dk) // dv + h
    return pl.pallas_call(
        _gla_kernel,
        grid=(batch, h),
        in_specs=[pl.BlockSpec((seq, dk), lambda b, i: (b, qb + i)),
                  pl.BlockSpec((seq, dk), lambda b, i: (b, kb + i)),
                  pl.BlockSpec((seq, dv), lambda b, i: (b, vb + i)),
                  pl.BlockSpec((seq, dv), lambda b, i: (b, zb + i)),
                  pl.BlockSpec((seq, LANES), lambda b, i: (b, 0)),
                  pl.BlockSpec((2, LANES, dk), lambda b, i: (0, 0, i)),
                  pl.BlockSpec((2, 1, dk), lambda b, i: (0, 0, i)),
                  pl.BlockSpec((1, dv), lambda b, i: (0, 0)),
                  pl.BlockSpec(cmats.shape, lambda b, i: (0, 0, 0))],
        out_specs=pl.BlockSpec((seq, dv), lambda b, i: (b, i)),
        out_shape=jax.ShapeDtypeStruct((m, width), BF16),
        scratch_shapes=[pltpu.VMEM((dv, seq), BF16),
                        pltpu.VMEM((2, dv, dk), F32),
                        pltpu.VMEM((seq, dv), F32)],
        compiler_params=_params("parallel", "parallel"),
        name="gla",
    )(proj, proj, proj, proj, lr, wa, ba, gnorm, cmats)


def _attn_kernel(lam_init, relb_ref, lam_ref, dn_ref, q_ref, k_ref, v_ref, z_ref, o_ref,
                 bias_scr, s_scr, m_scr, vt_scr):
    h = pl.program_id(0)
    s_len, d2 = q_ref.shape
    d = d2 // 2
    tq = bias_scr.shape[1]
    bt = ATTN_BIAS_ROWS
    n_tiles = bias_scr.shape[0] // bt

    @pl.when(pl.program_id(1) == 0)
    def _():
        def tile(t, carry):
            c0 = pl.multiple_of(t * bt, bt)
            rel = (lax.broadcasted_iota(jnp.int32, (bt, tq), 0) + (c0 - (s_len - tq))
                   - lax.broadcasted_iota(jnp.int32, (bt, tq), 1))
            dist = jnp.abs(rel)
            neg = jnp.full((bt, tq), relb_ref[0, h], F32)
            pos = jnp.full((bt, tq), relb_ref[REL_BUCKETS // 2, h], F32)
            for bucket, start in enumerate(T5_STARTS, start=1):
                far = dist >= start
                neg = jnp.where(far, relb_ref[bucket, h], neg)
                pos = jnp.where(far, relb_ref[REL_BUCKETS // 2 + bucket, h], pos)
            bias_scr[pl.ds(c0, bt), :] = jnp.where(rel > 0, pos, neg) * LOG2E
            return carry
        lax.fori_loop(0, n_tiles, tile, 0)

    vt_scr[:d2, :] = v_ref[...].astype(F32).T.astype(BF16)
    vt_scr[d2:, :] = jnp.ones((vt_scr.shape[0] - d2, s_len), BF16)

    lv = lam_ref[...]
    lam = (jnp.exp(jnp.sum(lv[0:1] * lv[1:2], axis=-1, keepdims=True))
           - jnp.exp(jnp.sum(lv[2:3] * lv[3:4], axis=-1, keepdims=True)) + lam_init)

    n_blocks = s_len // tq

    def logits_stage(i, slot):
        r0 = pl.multiple_of(i * tq, tq)
        bias = bias_scr[pl.ds(pl.multiple_of(s_len - tq - r0, tq), s_len), :]
        q = (q_ref[pl.ds(r0, tq), :].astype(F32) * (d ** -0.5 * LOG2E)).astype(BF16)
        for m in range(2):
            t = _dot_nt(k_ref[:, m * d:(m + 1) * d], q[:, m * d:(m + 1) * d]) + bias
            s_scr[slot, m] = t
            m_scr[slot, m] = jnp.max(t, axis=0, keepdims=True)

    def softmax_stage(i, slot):
        rows = pl.ds(pl.multiple_of(i * tq, tq), tq)
        num, den = [], []
        for m in range(2):
            p = jnp.exp2(s_scr[slot, m] - m_scr[slot, m]).astype(BF16)
            r = _dot(vt_scr[...], p)
            num.append(r[:d2])
            den.append(r[d2:d2 + 1])
        o = (num[0] / den[0] - num[1] * (lam / den[1])).T
        y = o * lax.rsqrt(jnp.mean(o * o, axis=-1, keepdims=True) + EPS) * dn_ref[...]
        y = y * (1.0 - lam_init) * _silu(z_ref[rows, :].astype(F32))
        o_ref[rows, :] = y.astype(o_ref.dtype)

    logits_stage(0, 0)
    if n_blocks % 2 == 0:
        def pair(p, carry):
            logits_stage(2 * p + 1, 1)
            softmax_stage(2 * p, 0)
            logits_stage(2 * p + 2, 0)
            softmax_stage(2 * p + 1, 1)
            return carry
        lax.fori_loop(0, n_blocks // 2 - 1, pair, 0)
        logits_stage(n_blocks - 1, 1)
        softmax_stage(n_blocks - 2, 0)
        softmax_stage(n_blocks - 1, 1)
    else:
        assert n_blocks == 1
        softmax_stage(0, 0)


def _attn_call(proj, rel_bias, lam_par, dnorm, lam_init, batch, seq, width, col0):
    m = proj.shape[0]
    nh = DIFF_HEADS
    hw = width // nh
    c0 = col0 // hw
    tq = min(ATTN_Q_ROWS, seq)
    blk = lambda c: pl.BlockSpec((seq, hw), lambda h, b: (b, c + h))
    return pl.pallas_call(
        functools.partial(_attn_kernel, lam_init),
        grid=(nh, batch),
        in_specs=[pl.BlockSpec(memory_space=pltpu.SMEM),
                  pl.BlockSpec(lam_par.shape, lambda h, b: (0, 0)),
                  pl.BlockSpec((1, hw), lambda h, b: (0, 0)),
                  blk(c0), blk(c0 + nh), blk(c0 + 2 * nh), blk(c0 + 3 * nh)],
        out_specs=pl.BlockSpec((seq, hw), lambda h, b: (b, h)),
        out_shape=jax.ShapeDtypeStruct((m, width), BF16),
        scratch_shapes=[pltpu.VMEM((2 * seq - tq, tq), F32),
                        pltpu.VMEM((2, 2, seq, tq), F32),
                        pltpu.VMEM((2, 2, 1, tq), F32),
                        pltpu.VMEM((hw + BF16_ROWS, seq), BF16)],
        compiler_params=_params("arbitrary", "arbitrary"),
        name="diff_attn",
    )(rel_bias, lam_par, dnorm, proj, proj, proj, proj)


def _merge_kernel(h_ref, za_ref, zb_ref, zc_ref, wm0_ref, wm1_ref, wm2_ref,
                  bm0_ref, bm1_ref, bm2_ref, wb0_ref, wb1_ref, wb2_ref, o_ref):
    hh = h_ref[...]
    acc = None
    for z_ref, wm_ref, bm_ref, wb_ref in ((za_ref, wm0_ref, bm0_ref, wb0_ref),
                                          (zb_ref, wm1_ref, bm1_ref, wb1_ref),
                                          (zc_ref, wm2_ref, bm2_ref, wb2_ref)):
        gate = _sigmoid(_dot(hh, wm_ref[...]) + bm_ref[...])
        term = gate * _dot(z_ref[...], wb_ref[0])
        acc = term if acc is None else acc + term
    o_ref[...] = acc.astype(o_ref.dtype)


def _merge_call(h, za, zb, zc, wm, bm, wb, layer):
    m, d = h.shape
    tm = min(512, m)
    tn = min(512, d)
    nj = d // tn
    row = pl.BlockSpec((tm, d), lambda j, i: (i, 0))
    wm_spec = lambda br: pl.BlockSpec((None, d, tn), lambda j, i: (layer, 0, br * nj + j))
    bm_spec = lambda br: pl.BlockSpec((None, 1, tn), lambda j, i: (layer, 0, br * nj + j))
    wb_spec = lambda br: pl.BlockSpec((None, 1, d, tn), lambda j, i: (layer, br, 0, j))
    return pl.pallas_call(
        _merge_kernel,
        grid=(nj, m // tm),
        in_specs=[row, row, row, row,
                  wm_spec(0), wm_spec(1), wm_spec(2),
                  bm_spec(0), bm_spec(1), bm_spec(2),
                  wb_spec(0), wb_spec(1), wb_spec(2)],
        out_specs=pl.BlockSpec((tm, tn), lambda j, i: (i, j)),
        out_shape=jax.ShapeDtypeStruct((m, d), BF16),
        compiler_params=_params("arbitrary", "arbitrary"),
        name="merge",
    )(h, za, zb, zc, wm, wm, wm, bm, bm, bm, wb, wb, wb)


def _out_kernel(mg_ref, w_ref, x_ref, g_ref, gn_ref, o_ref, *maybe_h_ref):
    out = _dot(mg_ref[...], w_ref[...])
    y = out * lax.rsqrt(jnp.mean(out * out, axis=-1, keepdims=True) + EPS) * g_ref[...]
    x = x_ref[...] + y
    o_ref[...] = x
    if maybe_h_ref:
        hn = x * lax.rsqrt(jnp.mean(x * x, axis=-1, keepdims=True) + EPS)
        maybe_h_ref[0][...] = (hn * gn_ref[...]).astype(BF16)


def _out_call(merged, w, layer, x2, g, g_next):
    m, d = x2.shape
    tm = min(512, m)
    row = pl.BlockSpec((tm, d), lambda i: (i, 0))
    vec = pl.BlockSpec((1, d), lambda i: (0, 0))
    with_h = g_next is not None
    out = pl.pallas_call(
        _out_kernel,
        grid=(m // tm,),
        in_specs=[row, pl.BlockSpec((None, d, d), lambda i: (layer, 0, 0)), row, vec, vec],
        out_specs=[row, row] if with_h else row,
        out_shape=([jax.ShapeDtypeStruct((m, d), F32), jax.ShapeDtypeStruct((m, d), BF16)]
                   if with_h else jax.ShapeDtypeStruct((m, d), F32)),
        compiler_params=_params("parallel"),
        name="out_proj",
    )(merged, w, x2, g.reshape(1, d), (g_next if with_h else g).reshape(1, d))
    return out if with_h else (out, None)


def kernel(x, norm_pre, w_in, gmlp_ln_g, gmlp_ln_b, gmlp_ws, gmlp_bs, gla_wa2, gla_ba, gla_norm,
           diff_lambda, diff_norm, rel_bias, w_branch, w_merge, b_merge, w_out, norm_post):
    batch, seq, d = x.shape
    depth = norm_pre.shape[0]
    width = d
    m = batch * seq
    hk = gla_wa2.shape[-1]
    lr_col = 3 * width + 2 * hk + 2 * width
    lr_w = 2 * GLA_RANK
    assert seq % GLA_CHUNK == 0 and seq % ATTN_Q_ROWS == 0 and seq % GMLP_CHUNK == 0
    assert w_in.shape[-1] == lr_col + lr_w + 4 * width

    cmats = jnp.asarray(_gla_constants(), BF16)
    w_t = jnp.swapaxes(w_in, 1, 2)
    ws_b = gmlp_ws.astype(BF16)
    wm_b = w_merge.astype(BF16)
    wb_b = w_branch.astype(BF16)
    wo_b = w_out.astype(BF16)
    bm = b_merge.reshape(depth, 1, -1)

    x2 = x.reshape(m, d)
    h = _rms_call(x2, norm_pre[0])
    for l in range(depth):
        wa = jnp.zeros((2, LANES, hk), F32)
        wa = wa.at[0, :GLA_RANK].set(gla_wa2[l, 0]).at[1, GLA_RANK:lr_w].set(gla_wa2[l, 1]).astype(BF16)
        ba = gla_ba[l].reshape(2, 1, hk)

        proj_ab = _mm_call(h, w_t, l, 0, lr_col, "in_proj_ab")
        proj_c = _mm_call(h, w_t, l, lr_col + lr_w, 4 * width, "in_proj_c")
        lr = _mm_call(h, w_t, l, lr_col, LANES, "lr_proj")

        za = _gmlp_call(proj_ab, gmlp_ln_g[l].reshape(1, width), gmlp_ln_b[l].reshape(1, width),
                        ws_b, gmlp_bs[l].T, l, width)
        zb = _gla_call(proj_ab, lr, wa, ba, gla_norm[l].reshape(1, -1), cmats, batch, seq, width,
                       3 * width)
        lam_init = 0.8 - 0.6 * math.exp(-0.3 * l)
        zc = _attn_call(proj_c, rel_bias, diff_lambda[l], diff_norm[l].reshape(1, -1), lam_init,
                        batch, seq, width, 0)
        merged = _merge_call(h, za, zb, zc, wm_b, bm, wb_b, l)
        x2, h = _out_call(merged, wo_b, l, x2, norm_post[l],
                          norm_pre[l + 1] if l + 1 < depth else None)
    return x2.reshape(batch, seq, d)
```

```python
import functools
import math

import numpy as np
import jax
import jax.numpy as jnp
from jax import lax
from jax.experimental import pallas as pl
from jax.experimental.pallas import tpu as pltpu

F32 = jnp.float32
BF16 = jnp.bfloat16

EPS = 1e-6
LOG2E = math.log2(math.e)
N_BRANCH = 3
GMLP_CHUNK = 128
GMLP_GROUPS = 8
GLA_HEADS = 4
GLA_RANK = 16
GLA_TAU = 16.0
GLA_SUB = 64
GLA_CHUNK = 2 * GLA_SUB
GLA_GROUP = 8
DIFF_HEADS = 8
ATTN_Q_ROWS = 256
ATTN_BIAS_ROWS = 32
REL_BUCKETS = 32
T5_STARTS = (1, 2, 3, 4, 5, 6, 7, 8, 12, 16, 23, 32, 46, 64, 91)

LANES = 128
SUBLANES = 8
BF16_ROWS = 2 * SUBLANES
VMEM_LIMIT = 52 * 1024 * 1024


def _params(*sem):
    return pltpu.CompilerParams(dimension_semantics=sem, vmem_limit_bytes=VMEM_LIMIT)


GELU_C = math.sqrt(2.0 / math.pi)


def _gelu(x):
    hx = 0.5 * x
    return hx + hx * jnp.tanh(x * (GELU_C + (GELU_C * 0.044715) * (x * x)))


def _sigmoid(x):
    return 0.5 + 0.5 * jnp.tanh(0.5 * x)


def _silu(x):
    hx = 0.5 * x
    return hx + hx * jnp.tanh(hx)


def _log2_sigmoid(x):
    u = x * LOG2E
    return jnp.minimum(u, 0.0) - jnp.log2(1.0 + jnp.exp2(-jnp.abs(u)))


def _dot(a, b):
    return jnp.dot(a, b, preferred_element_type=F32)


def _dot_nt(a, b):
    return lax.dot_general(a, b, (((1,), (1,)), ((), ())), preferred_element_type=F32)


def _rms_kernel(x_ref, g_ref, o_ref):
    x = x_ref[...]
    y = x * lax.rsqrt(jnp.mean(x * x, axis=-1, keepdims=True) + EPS)
    o_ref[...] = (y * g_ref[...]).astype(o_ref.dtype)


def _rms_call(x2, g):
    m, d = x2.shape
    tm = min(512, m)
    return pl.pallas_call(
        _rms_kernel,
        grid=(m // tm,),
        in_specs=[pl.BlockSpec((tm, d), lambda i: (i, 0)),
                  pl.BlockSpec((1, d), lambda i: (0, 0))],
        out_specs=pl.BlockSpec((tm, d), lambda i: (i, 0)),
        out_shape=jax.ShapeDtypeStruct((m, d), BF16),
        compiler_params=_params("parallel"),
        name="rms_pre",
    )(x2, g.reshape(1, d))


def _mm_kernel(a_ref, w_ref, o_ref, w_scr):
    @pl.when(pl.program_id(1) == 0)
    def _():
        w_scr[...] = w_ref[0].astype(BF16)
    o_ref[...] = _dot_nt(a_ref[...], w_scr[...]).astype(o_ref.dtype)


def _mm_call(a, w_t, layer, row0, n, name):
    m, k = a.shape
    tm = min(1024, m)
    tn = min(1024, n)
    return pl.pallas_call(
        _mm_kernel,
        grid=(n // tn, m // tm),
        in_specs=[pl.BlockSpec((tm, k), lambda j, i: (i, 0)),
                  pl.BlockSpec((pl.Element(1), pl.Element(tn), pl.Element(k)),
                               lambda j, i: (layer, pl.multiple_of(row0 + j * tn, SUBLANES), 0))],
        out_specs=pl.BlockSpec((tm, tn), lambda j, i: (i, j)),
        out_shape=jax.ShapeDtypeStruct((m, n), BF16),
        scratch_shapes=[pltpu.VMEM((tn, k), BF16)],
        compiler_params=_params("arbitrary", "arbitrary"),
        name=name,
    )(a, w_t)


def _gmlp_kernel(u_ref, v_ref, z_ref, lng_ref, lnb_ref, ws_ref, bs_ref, o_ref):
    tm, w = v_ref.shape
    gc = w // GMLP_GROUPS
    gv = _gelu(v_ref[...].astype(F32))
    mu = jnp.mean(gv, axis=-1, keepdims=True)
    cen = gv - mu
    var = jnp.mean(cen * cen, axis=-1, keepdims=True)
    sv = (cen * lax.rsqrt(var + EPS) * lng_ref[...] + lnb_ref[...]).astype(BF16)
    for c in range(tm // GMLP_CHUNK):
        rows = slice(c * GMLP_CHUNK, (c + 1) * GMLP_CHUNK)
        for g in range(GMLP_GROUPS):
            cols = slice(g * gc, (g + 1) * gc)
            mixed = _dot(ws_ref[g], sv[rows, cols]) + bs_ref[:, g:g + 1]
            u = _gelu(u_ref[rows, cols].astype(F32))
            z = _silu(z_ref[rows, cols].astype(F32))
            o_ref[rows, cols] = (u * mixed * z).astype(o_ref.dtype)


def _gmlp_call(proj, lng, lnb, ws, bs_t, layer, width):
    m = proj.shape[0]
    tm = min(256, m)
    blk = lambda c: pl.BlockSpec((tm, width), lambda i: (i, c))
    full = lambda a: pl.BlockSpec(a.shape, lambda i: (0,) * a.ndim)
    return pl.pallas_call(
        _gmlp_kernel,
        grid=(m // tm,),
        in_specs=[blk(0), blk(1), blk(2), full(lng), full(lnb),
                  pl.BlockSpec((None,) + ws.shape[1:], lambda i: (layer, 0, 0, 0)), full(bs_t)],
        out_specs=pl.BlockSpec((tm, width), lambda i: (i, 0)),
        out_shape=jax.ShapeDtypeStruct((m, width), BF16),
        compiler_params=_params("parallel"),
        name="gmlp",
    )(proj, proj, proj, lng, lnb, ws, bs_t)


def _mm_gmlp_kernel(a_ref, w_ref, u_ref, v_ref, z_ref, lng_ref, lnb_ref, ws_ref, bs_ref,
                    o_ref, za_ref, w_scr):
    _mm_kernel(a_ref, w_ref, o_ref, w_scr)
    _gmlp_kernel(u_ref, v_ref, z_ref, lng_ref, lnb_ref, ws_ref, bs_ref, za_ref)


def _mm_gmlp_rows(m, n):
    steps = (n // min(1024, n)) * (m // min(1024, m))
    return m // steps if m % (steps * GMLP_CHUNK) == 0 else 0


def _mm_gmlp_call(a, w_t, layer, row0, n, proj, lng, lnb, ws, bs_t, width, name):
    m, k = a.shape
    tm = min(1024, m)
    tn = min(1024, n)
    ni = m // tm
    gr = _mm_gmlp_rows(m, n)
    slab = lambda c: pl.BlockSpec((gr, width), lambda j, i: (j * ni + i, c))
    full = lambda x: pl.BlockSpec(x.shape, lambda j, i: (0,) * x.ndim)
    return pl.pallas_call(
        _mm_gmlp_kernel,
        grid=(n // tn, ni),
        in_specs=[pl.BlockSpec((tm, k), lambda j, i: (i, 0)),
                  pl.BlockSpec((pl.Element(1), pl.Element(tn), pl.Element(k)),
                               lambda j, i: (layer, pl.multiple_of(row0 + j * tn, SUBLANES), 0)),
                  slab(0), slab(1), slab(2), full(lng), full(lnb),
                  pl.BlockSpec((None,) + ws.shape[1:], lambda j, i: (layer, 0, 0, 0)), full(bs_t)],
        out_specs=[pl.BlockSpec((tm, tn), lambda j, i: (i, j)), slab(0)],
        out_shape=[jax.ShapeDtypeStruct((m, n), BF16), jax.ShapeDtypeStruct((m, width), BF16)],
        scratch_shapes=[pltpu.VMEM((tn, k), BF16)],
        compiler_params=_params("arbitrary", "arbitrary"),
        name=name,
    )(a, w_t, proj, proj, proj, lng, lnb, ws, bs_t)


def _gla_constants():
    c, s = GLA_CHUNK, GLA_SUB
    i = np.arange(c)[:, None]
    j = np.arange(c)[None, :]
    same = (i // s) == (j // s)
    mats = []
    for rev in (False, True):
        if not rev:
            cum = same & (j <= i)
            ref = same & ((j % s) <= s // 2 - 1)
        else:
            cum = same & (j >= i)
            ref = same & ((j % s) >= s // 2)
        cum, ref, tot = (t.astype(np.float32) for t in (cum, ref, same))
        blocks = np.concatenate([cum, cum - ref, ref - cum, tot - cum], axis=0)
        mats.append(np.concatenate([blocks, blocks], axis=1))
    return np.stack(mats)


def _gla_kernel(q_ref, k_ref, v_ref, z_ref, lr_ref, wa_ref, ba_ref, gn_ref, cm_ref, o_ref,
                vt_scr, st_scr, o_scr):
    s_len, dk = q_ref.shape
    c, sub = GLA_CHUNK, GLA_SUB
    n_chunks = s_len // c

    vt_scr[...] = v_ref[...].astype(F32).T.astype(BF16)
    st_scr[...] = jnp.zeros_like(st_scr)

    ri = lax.broadcasted_iota(jnp.int32, (c, c), 0)
    ci = lax.broadcasted_iota(jnp.int32, (c, c), 1)
    same = (ri // sub) == (ci // sub)
    row_in_sub0 = lax.broadcasted_iota(jnp.int32, (c, 1), 0) < sub

    n_grp = min(GLA_GROUP, n_chunks)
    assert n_chunks % n_grp == 0

    def group(d, gi, first):
        rev = d == 1
        cs = range(n_grp)
        base = pl.multiple_of(gi * (n_grp * c), n_grp * c)
        rows = pl.ds(base, n_grp * c)
        crow = [pl.ds(pl.multiple_of(base + i * c, c), c) for i in cs]
        part = lambda x, i: x[i * c:(i + 1) * c]
        q = q_ref[rows, :].astype(F32) * (dk ** -0.5)
        k = k_ref[rows, :].astype(F32)
        g = _log2_sigmoid(_dot(lr_ref[rows, :], wa_ref[d]) + ba_ref[d]) * (1.0 / GLA_TAU)
        g_hi = g.astype(BF16)
        g_lo = (g - g_hi.astype(F32)).astype(BF16)
        e = [_dot(cm_ref[d], jnp.concatenate([part(g_hi, i), part(g_lo, i)], axis=0)) for i in cs]
        cum = [e[i][:c] for i in cs]
        if rev:
            last0, last1 = [x[0:1] for x in cum], [x[sub:sub + 1] for x in cum]
            diag_mask = same & (ci >= ri)
            off_mask = (ri < sub) & (ci >= sub)
        else:
            last0, last1 = [x[sub - 1:sub] for x in cum], [x[c - 1:c] for x in cum]
            diag_mask = same & (ci <= ri)
            off_mask = (ri >= sub) & (ci < sub)
        q_in = [(part(q, i) * jnp.exp2(e[i][c:2 * c])).astype(BF16) for i in cs]
        k_in = [(part(k, i) * jnp.exp2(e[i][2 * c:3 * c])).astype(BF16) for i in cs]
        q_dec = [part(q, i) * jnp.exp2(cum[i]) for i in cs]
        k_dec = [part(k, i) * jnp.exp2(e[i][3 * c:]) for i in cs]
        s_diag = [_dot_nt(q_in[i], k_in[i]) for i in cs]
        s_off = [_dot_nt(q_dec[i].astype(BF16), k_dec[i].astype(BF16)) for i in cs]
        scores = [(jnp.where(diag_mask, s_diag[i], 0.0)
                   + jnp.where(off_mask, s_off[i], 0.0)).astype(BF16) for i in cs]
        o_intra = [_dot(scores[i], v_ref[crow[i], :]) for i in cs]
        if rev:
            q_st = [q_dec[i] * jnp.where(row_in_sub0, jnp.exp2(last1[i]), 1.0) for i in cs]
            k_st = [k_dec[i] * jnp.where(row_in_sub0, 1.0, jnp.exp2(last0[i])) for i in cs]
        else:
            q_st = [q_dec[i] * jnp.where(row_in_sub0, 1.0, jnp.exp2(last0[i])) for i in cs]
            k_st = [k_dec[i] * jnp.where(row_in_sub0, jnp.exp2(last1[i]), 1.0) for i in cs]
        kv_t = [_dot(vt_scr[:, crow[i]], k_st[i].astype(BF16)) for i in cs]
        st = st_scr[d]
        seen = [None] * n_grp
        for i in (reversed(cs) if rev else cs):
            seen[i] = st.astype(BF16)
            st = st * jnp.exp2(last0[i] + last1[i]) + kv_t[i]
        st_scr[d] = st
        for i in cs:
            o = o_intra[i] + _dot_nt(q_st[i].astype(BF16), seen[i])
            if first:
                o_scr[crow[i], :] = o
            else:
                o_scr[crow[i], :] += o

    n_groups = n_chunks // n_grp
    half = n_groups // 2

    def sweep(first):
        def body(t, carry):
            group(0, t, first)
            group(1, n_groups - 1 - t, first)
            return carry
        return body

    if n_groups == 1:
        group(0, 0, True)
        group(1, 0, False)
    elif n_groups % 2 == 0:
        lax.fori_loop(0, half, sweep(True), 0)
        lax.fori_loop(half, n_groups, sweep(False), 0)
    else:
        o_scr[...] = jnp.zeros_like(o_scr)
        lax.fori_loop(0, n_groups, sweep(False), 0)

    o = o_scr[...]
    y = o * lax.rsqrt(jnp.mean(o * o, axis=-1, keepdims=True) + EPS) * gn_ref[...]
    o_ref[...] = (y * _silu(z_ref[...].astype(F32))).astype(o_ref.dtype)


def _gla_call(proj, lr, wa, ba, gnorm, cmats, batch, seq, width, col0):
    m = proj.shape[0]
    h = GLA_HEADS
    dv = width // h
    dk = dv // 2
    qb, kb = col0 // dk, col0 // dk + h
    vb, zb = (col0 + 2 * h * dk) // dv, (col0 + 2 * h * dk) // dv + h
    return pl.pallas_call(
        _gla_kernel,
        grid=(batch, h),
        in_specs=[pl.BlockSpec((seq, dk), lambda b, i: (b, qb + i)),
                  pl.BlockSpec((seq, dk), lambda b, i: (b, kb + i)),
                  pl.BlockSpec((seq, dv), lambda b, i: (b, vb + i)),
                  pl.BlockSpec((seq, dv), lambda b, i: (b, zb + i)),
                  pl.BlockSpec((seq, LANES), lambda b, i: (b, 0)),
                  pl.BlockSpec((2, LANES, dk), lambda b, i: (0, 0, i)),
                  pl.BlockSpec((2, 1, dk), lambda b, i: (0, 0, i)),
                  pl.BlockSpec((1, dv), lambda b, i: (0, 0)),
                  pl.BlockSpec(cmats.shape, lambda b, i: (0, 0, 0))],
        out_specs=pl.BlockSpec((seq, dv), lambda b, i: (b, i)),
        out_shape=jax.ShapeDtypeStruct((m, width), BF16),
        scratch_shapes=[pltpu.VMEM((dv, seq), BF16),
                        pltpu.VMEM((2, dv, dk), F32),
                        pltpu.VMEM((seq, dv), F32)],
        compiler_params=_params("parallel", "parallel"),
        name="gla",
    )(proj, proj, proj, proj, lr, wa, ba, gnorm, cmats)


def _attn_kernel(lam_init, relb_ref, lam_ref, dn_ref, q_ref, k_ref, v_ref, z_ref, o_ref,
                 bias_scr, s_scr, m_scr, vt_scr):
    h = pl.program_id(0)
    s_len, d2 = q_ref.shape
    d = d2 // 2
    tq = bias_scr.shape[1]
    bt = ATTN_BIAS_ROWS
    n_tiles = bias_scr.shape[0] // bt

    @pl.when(pl.program_id(1) == 0)
    def _():
        def tile(t, carry):
            c0 = pl.multiple_of(t * bt, bt)
            rel = (lax.broadcasted_iota(jnp.int32, (bt, tq), 0) + (c0 - (s_len - tq))
                   - lax.broadcasted_iota(jnp.int32, (bt, tq), 1))
            dist = jnp.abs(rel)
            neg = jnp.full((bt, tq), relb_ref[0, h], F32)
            pos = jnp.full((bt, tq), relb_ref[REL_BUCKETS // 2, h], F32)
            for bucket, start in enumerate(T5_STARTS, start=1):
                far = dist >= start
                neg = jnp.where(far, relb_ref[bucket, h], neg)
                pos = jnp.where(far, relb_ref[REL_BUCKETS // 2 + bucket, h], pos)
            bias_scr[pl.ds(c0, bt), :] = jnp.where(rel > 0, pos, neg) * LOG2E
            return carry
        lax.fori_loop(0, n_tiles, tile, 0)

    vt_scr[:d2, :] = v_ref[...].astype(F32).T.astype(BF16)
    vt_scr[d2:, :] = jnp.ones((vt_scr.shape[0] - d2, s_len), BF16)

    lv = lam_ref[...]
    lam = (jnp.exp(jnp.sum(lv[0:1] * lv[1:2], axis=-1, keepdims=True))
           - jnp.exp(jnp.sum(lv[2:3] * lv[3:4], axis=-1, keepdims=True)) + lam_init)

    n_blocks = s_len // tq

    def logits_stage(i, slot):
        r0 = pl.multiple_of(i * tq, tq)
        bias = bias_scr[pl.ds(pl.multiple_of(s_len - tq - r0, tq), s_len), :]
        q = (q_ref[pl.ds(r0, tq), :].astype(F32) * (d ** -0.5 * LOG2E)).astype(BF16)
        for m in range(2):
            t = _dot_nt(k_ref[:, m * d:(m + 1) * d], q[:, m * d:(m + 1) * d]) + bias
            s_scr[slot, m] = t
            m_scr[slot, m] = jnp.max(t, axis=0, keepdims=True)

    def softmax_stage(i, slot):
        rows = pl.ds(pl.multiple_of(i * tq, tq), tq)
        num, den = [], []
        for m in range(2):
            p = jnp.exp2(s_scr[slot, m] - m_scr[slot, m]).astype(BF16)
            r = _dot(vt_scr[...], p)
            num.append(r[:d2])
            den.append(r[d2:d2 + 1])
        o = (num[0] / den[0] - num[1] * (lam / den[1])).T
        y = o * lax.rsqrt(jnp.mean(o * o, axis=-1, keepdims=True) + EPS) * dn_ref[...]
        y = y * (1.0 - lam_init) * _silu(z_ref[rows, :].astype(F32))
        o_ref[rows, :] = y.astype(o_ref.dtype)

    logits_stage(0, 0)
    if n_blocks % 2 == 0:
        def pair(p, carry):
            logits_stage(2 * p + 1, 1)
            softmax_stage(2 * p, 0)
            logits_stage(2 * p + 2, 0)
            softmax_stage(2 * p + 1, 1)
            return carry
        lax.fori_loop(0, n_blocks // 2 - 1, pair, 0)
        logits_stage(n_blocks - 1, 1)
        softmax_stage(n_blocks - 2, 0)
        softmax_stage(n_blocks - 1, 1)
    else:
        assert n_blocks == 1
        softmax_stage(0, 0)


def _attn_call(proj, rel_bias, lam_par, dnorm, lam_init, batch, seq, width, col0):
    m = proj.shape[0]
    nh = DIFF_HEADS
    hw = width // nh
    c0 = col0 // hw
    tq = min(ATTN_Q_ROWS, seq)
    blk = lambda c: pl.BlockSpec((seq, hw), lambda h, b: (b, c + h))
    return pl.pallas_call(
        functools.partial(_attn_kernel, lam_init),
        grid=(nh, batch),
        in_specs=[pl.BlockSpec(memory_space=pltpu.SMEM),
                  pl.BlockSpec(lam_par.shape, lambda h, b: (0, 0)),
                  pl.BlockSpec((1, hw), lambda h, b: (0, 0)),
                  blk(c0), blk(c0 + nh), blk(c0 + 2 * nh), blk(c0 + 3 * nh)],
        out_specs=pl.BlockSpec((seq, hw), lambda h, b: (b, h)),
        out_shape=jax.ShapeDtypeStruct((m, width), BF16),
        scratch_shapes=[pltpu.VMEM((2 * seq - tq, tq), F32),
                        pltpu.VMEM((2, 2, seq, tq), F32),
                        pltpu.VMEM((2, 2, 1, tq), F32),
                        pltpu.VMEM((hw + BF16_ROWS, seq), BF16)],
        compiler_params=_params("arbitrary", "arbitrary"),
        name="diff_attn",
    )(rel_bias, lam_par, dnorm, proj, proj, proj, proj)


def _merge_kernel(h_ref, za_ref, zb_ref, zc_ref, wm0_ref, wm1_ref, wm2_ref,
                  bm0_ref, bm1_ref, bm2_ref, wb0_ref, wb1_ref, wb2_ref, o_ref):
    hh = h_ref[...]
    acc = None
    for z_ref, wm_ref, bm_ref, wb_ref in ((za_ref, wm0_ref, bm0_ref, wb0_ref),
                                          (zb_ref, wm1_ref, bm1_ref, wb1_ref),
                                          (zc_ref, wm2_ref, bm2_ref, wb2_ref)):
        gate = _sigmoid(_dot(hh, wm_ref[...]) + bm_ref[...])
        term = gate * _dot(z_ref[...], wb_ref[0])
        acc = term if acc is None else acc + term
    o_ref[...] = acc.astype(o_ref.dtype)


def _merge_call(h, za, zb, zc, wm, bm, wb, layer):
    m, d = h.shape
    tm = min(512, m)
    tn = min(512, d)
    nj = d // tn
    row = pl.BlockSpec((tm, d), lambda j, i: (i, 0))
    wm_spec = lambda br: pl.BlockSpec((None, d, tn), lambda j, i: (layer, 0, br * nj + j))
    bm_spec = lambda br: pl.BlockSpec((None, 1, tn), lambda j, i: (layer, 0, br * nj + j))
    wb_spec = lambda br: pl.BlockSpec((None, 1, d, tn), lambda j, i: (layer, br, 0, j))
    return pl.pallas_call(
        _merge_kernel,
        grid=(nj, m // tm),
        in_specs=[row, row, row, row,
                  wm_spec(0), wm_spec(1), wm_spec(2),
                  bm_spec(0), bm_spec(1), bm_spec(2),
                  wb_spec(0), wb_spec(1), wb_spec(2)],
        out_specs=pl.BlockSpec((tm, tn), lambda j, i: (i, j)),
        out_shape=jax.ShapeDtypeStruct((m, d), BF16),
        compiler_params=_params("arbitrary", "arbitrary"),
        name="merge",
    )(h, za, zb, zc, wm, wm, wm, bm, bm, bm, wb, wb, wb)


def _out_kernel(mg_ref, w_ref, x_ref, g_ref, gn_ref, o_ref, *maybe_h_ref):
    out = _dot(mg_ref[...], w_ref[...])
    y = out * lax.rsqrt(jnp.mean(out * out, axis=-1, keepdims=True) + EPS) * g_ref[...]
    x = x_ref[...] + y
    o_ref[...] = x
    if maybe_h_ref:
        hn = x * lax.rsqrt(jnp.mean(x * x, axis=-1, keepdims=True) + EPS)
        maybe_h_ref[0][...] = (hn * gn_ref[...]).astype(BF16)


def _out_call(merged, w, layer, x2, g, g_next):
    m, d = x2.shape
    tm = min(512, m)
    row = pl.BlockSpec((tm, d), lambda i: (i, 0))
    vec = pl.BlockSpec((1, d), lambda i: (0, 0))
    with_h = g_next is not None
    out = pl.pallas_call(
        _out_kernel,
        grid=(m // tm,),
        in_specs=[row, pl.BlockSpec((None, d, d), lambda i: (layer, 0, 0)), row, vec, vec],
        out_specs=[row, row] if with_h else row,
        out_shape=([jax.ShapeDtypeStruct((m, d), F32), jax.ShapeDtypeStruct((m, d), BF16)]
                   if with_h else jax.ShapeDtypeStruct((m, d), F32)),
        compiler_params=_params("parallel"),
        name="out_proj",
    )(merged, w, x2, g.reshape(1, d), (g_next if with_h else g).reshape(1, d))
    return out if with_h else (out, None)


def kernel(x, norm_pre, w_in, gmlp_ln_g, gmlp_ln_b, gmlp_ws, gmlp_bs, gla_wa2, gla_ba, gla_norm,
           diff_lambda, diff_norm, rel_bias, w_branch, w_merge, b_merge, w_out, norm_post):
    batch, seq, d = x.shape
    depth = norm_pre.shape[0]
    width = d
    m = batch * seq
    hk = gla_wa2.shape[-1]
    lr_col = 3 * width + 2 * hk + 2 * width
    lr_w = 2 * GLA_RANK
    assert seq % GLA_CHUNK == 0 and seq % ATTN_Q_ROWS == 0 and seq % GMLP_CHUNK == 0
    assert w_in.shape[-1] == lr_col + lr_w + 4 * width

    cmats = jnp.asarray(_gla_constants(), BF16)
    w_t = jnp.swapaxes(w_in, 1, 2)
    ws_b = gmlp_ws.astype(BF16)
    wm_b = w_merge.astype(BF16)
    wb_b = w_branch.astype(BF16)
    wo_b = w_out.astype(BF16)
    bm = b_merge.reshape(depth, 1, -1)

    x2 = x.reshape(m, d)
    h = _rms_call(x2, norm_pre[0])
    for l in range(depth):
        wa = jnp.zeros((2, LANES, hk), F32)
        wa = wa.at[0, :GLA_RANK].set(gla_wa2[l, 0]).at[1, GLA_RANK:lr_w].set(gla_wa2[l, 1]).astype(BF16)
        ba = gla_ba[l].reshape(2, 1, hk)

        proj_ab = _mm_call(h, w_t, l, 0, lr_col, "in_proj_ab")
        lr = _mm_call(h, w_t, l, lr_col, LANES, "lr_proj")
        gmlp_args = (proj_ab, gmlp_ln_g[l].reshape(1, width), gmlp_ln_b[l].reshape(1, width),
                     ws_b, gmlp_bs[l].T)
        if _mm_gmlp_rows(m, 4 * width):
            proj_c, za = _mm_gmlp_call(h, w_t, l, lr_col + lr_w, 4 * width, *gmlp_args, width,
                                       "in_proj_c_gmlp")
        else:
            proj_c = _mm_call(h, w_t, l, lr_col + lr_w, 4 * width, "in_proj_c")
            za = _gmlp_call(*gmlp_args, l, width)
        zb = _gla_call(proj_ab, lr, wa, ba, gla_norm[l].reshape(1, -1), cmats, batch, seq, width,
                       3 * width)
        lam_init = 0.8 - 0.6 * math.exp(-0.3 * l)
        zc = _attn_call(proj_c, rel_bias, diff_lambda[l], diff_norm[l].reshape(1, -1), lam_init,
                        batch, seq, width, 0)
        merged = _merge_call(h, za, zb, zc, wm_b, bm, wb_b, l)
        x2, h = _out_call(merged, wo_b, l, x2, norm_post[l],
                          norm_pre[l + 1] if l + 1 < depth else None)
    return x2.reshape(batch, seq, d)
```

```python
import functools
import math

import numpy as np
import jax
import jax.numpy as jnp
from jax import lax
from jax.experimental import pallas as pl
from jax.experimental.pallas import tpu as pltpu

F32 = jnp.float32
BF16 = jnp.bfloat16

EPS = 1e-6
LOG2E = math.log2(math.e)
N_BRANCH = 3
GMLP_CHUNK = 128
GMLP_GROUPS = 8
GLA_HEADS = 4
GLA_RANK = 16
GLA_TAU = 16.0
GLA_SUB = 64
GLA_CHUNK = 2 * GLA_SUB
GLA_GROUP = 8
DIFF_HEADS = 8
ATTN_Q_ROWS = 256
ATTN_BIAS_ROWS = 32
REL_BUCKETS = 32
T5_STARTS = (1, 2, 3, 4, 5, 6, 7, 8, 12, 16, 23, 32, 46, 64, 91)

LANES = 128
SUBLANES = 8
BF16_ROWS = 2 * SUBLANES
VMEM_LIMIT = 52 * 1024 * 1024


def _params(*sem):
    return pltpu.CompilerParams(dimension_semantics=sem, vmem_limit_bytes=VMEM_LIMIT)


GELU_C = math.sqrt(2.0 / math.pi)


def _gelu(x):
    hx = 0.5 * x
    return hx + hx * jnp.tanh(x * (GELU_C + (GELU_C * 0.044715) * (x * x)))


def _sigmoid(x):
    return 0.5 + 0.5 * jnp.tanh(0.5 * x)


def _silu(x):
    hx = 0.5 * x
    return hx + hx * jnp.tanh(hx)


def _log2_sigmoid(x):
    u = x * LOG2E
    return jnp.minimum(u, 0.0) - jnp.log2(1.0 + jnp.exp2(-jnp.abs(u)))


def _dot(a, b):
    return jnp.dot(a, b, preferred_element_type=F32)


def _dot_nt(a, b):
    return lax.dot_general(a, b, (((1,), (1,)), ((), ())), preferred_element_type=F32)


def _rms_kernel(x_ref, g_ref, o_ref):
    x = x_ref[...]
    y = x * lax.rsqrt(jnp.mean(x * x, axis=-1, keepdims=True) + EPS)
    o_ref[...] = (y * g_ref[...]).astype(o_ref.dtype)


def _rms_call(x2, g):
    m, d = x2.shape
    tm = min(512, m)
    return pl.pallas_call(
        _rms_kernel,
        grid=(m // tm,),
        in_specs=[pl.BlockSpec((tm, d), lambda i: (i, 0)),
                  pl.BlockSpec((1, d), lambda i: (0, 0))],
        out_specs=pl.BlockSpec((tm, d), lambda i: (i, 0)),
        out_shape=jax.ShapeDtypeStruct((m, d), BF16),
        compiler_params=_params("parallel"),
        name="rms_pre",
    )(x2, g.reshape(1, d))


def _mm_kernel(a_ref, w_ref, o_ref, w_scr):
    @pl.when(pl.program_id(1) == 0)
    def _():
        w_scr[...] = w_ref[0].astype(BF16)
    o_ref[...] = _dot_nt(a_ref[...], w_scr[...]).astype(o_ref.dtype)


def _mm_call(a, w_t, layer, row0, n, name):
    m, k = a.shape
    tm = min(1024, m)
    tn = min(1024, n)
    return pl.pallas_call(
        _mm_kernel,
        grid=(n // tn, m // tm),
        in_specs=[pl.BlockSpec((tm, k), lambda j, i: (i, 0)),
                  pl.BlockSpec((pl.Element(1), pl.Element(tn), pl.Element(k)),
                               lambda j, i: (layer, pl.multiple_of(row0 + j * tn, SUBLANES), 0))],
        out_specs=pl.BlockSpec((tm, tn), lambda j, i: (i, j)),
        out_shape=jax.ShapeDtypeStruct((m, n), BF16),
        scratch_shapes=[pltpu.VMEM((tn, k), BF16)],
        compiler_params=_params("arbitrary", "arbitrary"),
        name=name,
    )(a, w_t)


def _mm_cast_kernel(a_ref, w_ref, wm_ref, wb_ref, wo_ref, o_ref, wm_o, wb_o, wo_o, w_scr):
    _mm_kernel(a_ref, w_ref, o_ref, w_scr)
    wm_o[...] = wm_ref[...].astype(BF16)
    wb_o[...] = wb_ref[...].astype(BF16)
    wo_o[...] = wo_ref[...].astype(BF16)


def _mm_cast_plan(m, n, wm, wb, wo):
    steps = (n // min(1024, n)) * (m // min(1024, m))
    if steps % N_BRANCH:
        return None
    third = steps // N_BRANCH
    rows_m, rows_b, rows_o = wm.shape[0] * wm.shape[1], math.prod(wb.shape[:3]), wo.shape[0] * wo.shape[1]
    if rows_m % third or rows_b % steps or rows_o % third:
        return None
    plan = (rows_m // third, rows_b // steps, rows_o // third)
    return plan if all(r % BF16_ROWS == 0 for r in plan) else None


def _mm_cast_call(a, w_t, layer, row0, n, wm, wb, wo, plan, name):
    m, k = a.shape
    d = wo.shape[-1]
    tm = min(1024, m)
    tn = min(1024, n)
    ni = m // tm
    steps = (n // tn) * ni
    third = steps // N_BRANCH
    rm, rb, ro = plan
    wm2, wb2, wo2 = wm.reshape(-1, N_BRANCH * d), wb.reshape(-1, d), wo.reshape(-1, d)
    step = lambda j, i: j * ni + i
    cast_specs = [pl.BlockSpec((rm, d), lambda j, i: (step(j, i) % third, step(j, i) // third)),
                  pl.BlockSpec((rb, d), lambda j, i: (step(j, i), 0)),
                  pl.BlockSpec((ro, d), lambda j, i: (jnp.minimum(step(j, i), third - 1), 0))]
    out = pl.pallas_call(
        _mm_cast_kernel,
        grid=(n // tn, ni),
        in_specs=[pl.BlockSpec((tm, k), lambda j, i: (i, 0)),
                  pl.BlockSpec((pl.Element(1), pl.Element(tn), pl.Element(k)),
                               lambda j, i: (layer, pl.multiple_of(row0 + j * tn, SUBLANES), 0))]
                 + cast_specs,
        out_specs=[pl.BlockSpec((tm, tn), lambda j, i: (i, j))] + cast_specs,
        out_shape=[jax.ShapeDtypeStruct((m, n), BF16)]
                  + [jax.ShapeDtypeStruct(w.shape, BF16) for w in (wm2, wb2, wo2)],
        scratch_shapes=[pltpu.VMEM((tn, k), BF16)],
        compiler_params=_params("arbitrary", "arbitrary"),
        name=name,
    )(a, w_t, wm2, wb2, wo2)
    return out[0], out[1].reshape(wm.shape), out[2].reshape(wb.shape), out[3].reshape(wo.shape)


def _gmlp_kernel(u_ref, v_ref, z_ref, lng_ref, lnb_ref, ws_ref, bs_ref, o_ref):
    tm, w = v_ref.shape
    gc = w // GMLP_GROUPS
    gv = _gelu(v_ref[...].astype(F32))
    mu = jnp.mean(gv, axis=-1, keepdims=True)
    cen = gv - mu
    var = jnp.mean(cen * cen, axis=-1, keepdims=True)
    sv = (cen * lax.rsqrt(var + EPS) * lng_ref[...] + lnb_ref[...]).astype(BF16)
    for c in range(tm // GMLP_CHUNK):
        rows = slice(c * GMLP_CHUNK, (c + 1) * GMLP_CHUNK)
        for g in range(GMLP_GROUPS):
            cols = slice(g * gc, (g + 1) * gc)
            mixed = _dot(ws_ref[g], sv[rows, cols]) + bs_ref[:, g:g + 1]
            u = _gelu(u_ref[rows, cols].astype(F32))
            z = _silu(z_ref[rows, cols].astype(F32))
            o_ref[rows, cols] = (u * mixed * z).astype(o_ref.dtype)


def _gmlp_call(proj, lng, lnb, ws, bs_t, layer, width):
    m = proj.shape[0]
    tm = min(256, m)
    blk = lambda c: pl.BlockSpec((tm, width), lambda i: (i, c))
    full = lambda a: pl.BlockSpec(a.shape, lambda i: (0,) * a.ndim)
    return pl.pallas_call(
        _gmlp_kernel,
        grid=(m // tm,),
        in_specs=[blk(0), blk(1), blk(2), full(lng), full(lnb),
                  pl.BlockSpec((None,) + ws.shape[1:], lambda i: (layer, 0, 0, 0)), full(bs_t)],
        out_specs=pl.BlockSpec((tm, width), lambda i: (i, 0)),
        out_shape=jax.ShapeDtypeStruct((m, width), BF16),
        compiler_params=_params("parallel"),
        name="gmlp",
    )(proj, proj, proj, lng, lnb, ws, bs_t)


def _mm_gmlp_kernel(a_ref, w_ref, u_ref, v_ref, z_ref, lng_ref, lnb_ref, ws_ref, bs_ref,
                    o_ref, za_ref, w_scr):
    _mm_kernel(a_ref, w_ref, o_ref, w_scr)
    _gmlp_kernel(u_ref, v_ref, z_ref, lng_ref, lnb_ref, ws_ref, bs_ref, za_ref)


def _mm_gmlp_rows(m, n):
    steps = (n // min(1024, n)) * (m // min(1024, m))
    return m // steps if m % (steps * GMLP_CHUNK) == 0 else 0


def _mm_gmlp_call(a, w_t, layer, row0, n, proj, lng, lnb, ws, bs_t, width, name):
    m, k = a.shape
    tm = min(1024, m)
    tn = min(1024, n)
    ni = m // tm
    gr = _mm_gmlp_rows(m, n)
    slab = lambda c: pl.BlockSpec((gr, width), lambda j, i: (j * ni + i, c))
    full = lambda x: pl.BlockSpec(x.shape, lambda j, i: (0,) * x.ndim)
    return pl.pallas_call(
        _mm_gmlp_kernel,
        grid=(n // tn, ni),
        in_specs=[pl.BlockSpec((tm, k), lambda j, i: (i, 0)),
                  pl.BlockSpec((pl.Element(1), pl.Element(tn), pl.Element(k)),
                               lambda j, i: (layer, pl.multiple_of(row0 + j * tn, SUBLANES), 0)),
                  slab(0), slab(1), slab(2), full(lng), full(lnb),
                  pl.BlockSpec((None,) + ws.shape[1:], lambda j, i: (layer, 0, 0, 0)), full(bs_t)],
        out_specs=[pl.BlockSpec((tm, tn), lambda j, i: (i, j)), slab(0)],
        out_shape=[jax.ShapeDtypeStruct((m, n), BF16), jax.ShapeDtypeStruct((m, width), BF16)],
        scratch_shapes=[pltpu.VMEM((tn, k), BF16)],
        compiler_params=_params("arbitrary", "arbitrary"),
        name=name,
    )(a, w_t, proj, proj, proj, lng, lnb, ws, bs_t)


def _gla_constants():
    c, s = GLA_CHUNK, GLA_SUB
    i = np.arange(c)[:, None]
    j = np.arange(c)[None, :]
    same = (i // s) == (j // s)
    mats = []
    for rev in (False, True):
        if not rev:
            cum = same & (j <= i)
            ref = same & ((j % s) <= s // 2 - 1)
        else:
            cum = same & (j >= i)
            ref = same & ((j % s) >= s // 2)
        cum, ref, tot = (t.astype(np.float32) for t in (cum, ref, same))
        blocks = np.concatenate([cum, cum - ref, ref - cum, tot - cum], axis=0)
        mats.append(np.concatenate([blocks, blocks], axis=1))
    return np.stack(mats)


def _gla_kernel(q_ref, k_ref, v_ref, z_ref, lr_ref, wa_ref, ba_ref, gn_ref, cm_ref, o_ref,
                vt_scr, st_scr, o_scr):
    s_len, dk = q_ref.shape
    c, sub = GLA_CHUNK, GLA_SUB
    n_chunks = s_len // c

    vt_scr[...] = v_ref[...].astype(F32).T.astype(BF16)
    st_scr[...] = jnp.zeros_like(st_scr)

    ri = lax.broadcasted_iota(jnp.int32, (c, c), 0)
    ci = lax.broadcasted_iota(jnp.int32, (c, c), 1)
    same = (ri // sub) == (ci // sub)
    row_in_sub0 = lax.broadcasted_iota(jnp.int32, (c, 1), 0) < sub

    n_grp = min(GLA_GROUP, n_chunks)
    assert n_chunks % n_grp == 0

    def group(d, gi, first):
        rev = d == 1
        cs = range(n_grp)
        base = pl.multiple_of(gi * (n_grp * c), n_grp * c)
        rows = pl.ds(base, n_grp * c)
        crow = [pl.ds(pl.multiple_of(base + i * c, c), c) for i in cs]
        part = lambda x, i: x[i * c:(i + 1) * c]
        q = q_ref[rows, :].astype(F32) * (dk ** -0.5)
        k = k_ref[rows, :].astype(F32)
        g = _log2_sigmoid(_dot(lr_ref[rows, :], wa_ref[d]) + ba_ref[d]) * (1.0 / GLA_TAU)
        g_hi = g.astype(BF16)
        g_lo = (g - g_hi.astype(F32)).astype(BF16)
        e = [_dot(cm_ref[d], jnp.concatenate([part(g_hi, i), part(g_lo, i)], axis=0)) for i in cs]
        cum = [e[i][:c] for i in cs]
        if rev:
            last0, last1 = [x[0:1] for x in cum], [x[sub:sub + 1] for x in cum]
            diag_mask = same & (ci >= ri)
            off_mask = (ri < sub) & (ci >= sub)
        else:
            last0, last1 = [x[sub - 1:sub] for x in cum], [x[c - 1:c] for x in cum]
            diag_mask = same & (ci <= ri)
            off_mask = (ri >= sub) & (ci < sub)
        q_in = [(part(q, i) * jnp.exp2(e[i][c:2 * c])).astype(BF16) for i in cs]
        k_in = [(part(k, i) * jnp.exp2(e[i][2 * c:3 * c])).astype(BF16) for i in cs]
        q_dec = [part(q, i) * jnp.exp2(cum[i]) for i in cs]
        k_dec = [part(k, i) * jnp.exp2(e[i][3 * c:]) for i in cs]
        s_diag = [_dot_nt(q_in[i], k_in[i]) for i in cs]
        s_off = [_dot_nt(q_dec[i].astype(BF16), k_dec[i].astype(BF16)) for i in cs]
        scores = [(jnp.where(diag_mask, s_diag[i], 0.0)
                   + jnp.where(off_mask, s_off[i], 0.0)).astype(BF16) for i in cs]
        o_intra = [_dot(scores[i], v_ref[crow[i], :]) for i in cs]
        if rev:
            q_st = [q_dec[i] * jnp.where(row_in_sub0, jnp.exp2(last1[i]), 1.0) for i in cs]
            k_st = [k_dec[i] * jnp.where(row_in_sub0, 1.0, jnp.exp2(last0[i])) for i in cs]
        else:
            q_st = [q_dec[i] * jnp.where(row_in_sub0, 1.0, jnp.exp2(last0[i])) for i in cs]
            k_st = [k_dec[i] * jnp.where(row_in_sub0, jnp.exp2(last1[i]), 1.0) for i in cs]
        kv_t = [_dot(vt_scr[:, crow[i]], k_st[i].astype(BF16)) for i in cs]
        st = st_scr[d]
        seen = [None] * n_grp
        for i in (reversed(cs) if rev else cs):
            seen[i] = st.astype(BF16)
            st = st * jnp.exp2(last0[i] + last1[i]) + kv_t[i]
        st_scr[d] = st
        for i in cs:
            o = o_intra[i] + _dot_nt(q_st[i].astype(BF16), seen[i])
            if first:
                o_scr[crow[i], :] = o
            else:
                o_scr[crow[i], :] += o

    n_groups = n_chunks // n_grp
    half = n_groups // 2

    def sweep(first):
        def body(t, carry):
            group(0, t, first)
            group(1, n_groups - 1 - t, first)
            return carry
        return body

    if n_groups == 1:
        group(0, 0, True)
        group(1, 0, False)
    elif n_groups % 2 == 0:
        lax.fori_loop(0, half, sweep(True), 0)
        lax.fori_loop(half, n_groups, sweep(False), 0)
    else:
        o_scr[...] = jnp.zeros_like(o_scr)
        lax.fori_loop(0, n_groups, sweep(False), 0)

    o = o_scr[...]
    y = o * lax.rsqrt(jnp.mean(o * o, axis=-1, keepdims=True) + EPS) * gn_ref[...]
    o_ref[...] = (y * _silu(z_ref[...].astype(F32))).astype(o_ref.dtype)


def _gla_call(proj, lr, wa, ba, gnorm, cmats, batch, seq, width, col0):
    m = proj.shape[0]
    h = GLA_HEADS
    dv = width // h
    dk = dv // 2
    qb, kb = col0 // dk, col0 // dk + h
    vb, zb = (col0 + 2 * h * dk) // dv, (col0 + 2 * h * dk) // dv + h
    return pl.pallas_call(
        _gla_kernel,
        grid=(batch, h),
        in_specs=[pl.BlockSpec((seq, dk), lambda b, i: (b, qb + i)),
                  pl.BlockSpec((seq, dk), lambda b, i: (b, kb + i)),
                  pl.BlockSpec((seq, dv), lambda b, i: (b, vb + i)),
                  pl.BlockSpec((seq, dv), lambda b, i: (b, zb + i)),
                  pl.BlockSpec((seq, LANES), lambda b, i: (b, 0)),
                  pl.BlockSpec((2, LANES, dk), lambda b, i: (0, 0, i)),
                  pl.BlockSpec((2, 1, dk), lambda b, i: (0, 0, i)),
                  pl.BlockSpec((1, dv), lambda b, i: (0, 0)),
                  pl.BlockSpec(cmats.shape, lambda b, i: (0, 0, 0))],
        out_specs=pl.BlockSpec((seq, dv), lambda b, i: (b, i)),
        out_shape=jax.ShapeDtypeStruct((m, width), BF16),
        scratch_shapes=[pltpu.VMEM((dv, seq), BF16),
                        pltpu.VMEM((2, dv, dk), F32),
                        pltpu.VMEM((seq, dv), F32)],
        compiler_params=_params("parallel", "parallel"),
        name="gla",
    )(proj, proj, proj, proj, lr, wa, ba, gnorm, cmats)


def _attn_kernel(lam_init, relb_ref, lam_ref, dn_ref, q_ref, k_ref, v_ref, z_ref, o_ref,
                 bias_scr, s_scr, m_scr, vt_scr):
    h = pl.program_id(0)
    s_len, d2 = q_ref.shape
    d = d2 // 2
    tq = bias_scr.shape[1]
    bt = ATTN_BIAS_ROWS
    n_tiles = bias_scr.shape[0] // bt

    @pl.when(pl.program_id(1) == 0)
    def _():
        def tile(t, carry):
            c0 = pl.multiple_of(t * bt, bt)
            rel = (lax.broadcasted_iota(jnp.int32, (bt, tq), 0) + (c0 - (s_len - tq))
                   - lax.broadcasted_iota(jnp.int32, (bt, tq), 1))
            dist = jnp.abs(rel)
            neg = jnp.full((bt, tq), relb_ref[0, h], F32)
            pos = jnp.full((bt, tq), relb_ref[REL_BUCKETS // 2, h], F32)
            for bucket, start in enumerate(T5_STARTS, start=1):
                far = dist >= start
                neg = jnp.where(far, relb_ref[bucket, h], neg)
                pos = jnp.where(far, relb_ref[REL_BUCKETS // 2 + bucket, h], pos)
            bias_scr[pl.ds(c0, bt), :] = jnp.where(rel > 0, pos, neg) * LOG2E
            return carry
        lax.fori_loop(0, n_tiles, tile, 0)

    vt_scr[:d2, :] = v_ref[...].astype(F32).T.astype(BF16)
    vt_scr[d2:, :] = jnp.ones((vt_scr.shape[0] - d2, s_len), BF16)

    lv = lam_ref[...]
    lam = (jnp.exp(jnp.sum(lv[0:1] * lv[1:2], axis=-1, keepdims=True))
           - jnp.exp(jnp.sum(lv[2:3] * lv[3:4], axis=-1, keepdims=True)) + lam_init)

    n_blocks = s_len // tq

    def logits_stage(i, slot):
        r0 = pl.multiple_of(i * tq, tq)
        bias = bias_scr[pl.ds(pl.multiple_of(s_len - tq - r0, tq), s_len), :]
        q = (q_ref[pl.ds(r0, tq), :].astype(F32) * (d ** -0.5 * LOG2E)).astype(BF16)
        for m in range(2):
            t = _dot_nt(k_ref[:, m * d:(m + 1) * d], q[:, m * d:(m + 1) * d]) + bias
            s_scr[slot, m] = t
            m_scr[slot, m] = jnp.max(t, axis=0, keepdims=True)

    def softmax_stage(i, slot):
        rows = pl.ds(pl.multiple_of(i * tq, tq), tq)
        num, den = [], []
        for m in range(2):
            p = jnp.exp2(s_scr[slot, m] - m_scr[slot, m]).astype(BF16)
            r = _dot(vt_scr[...], p)
            num.append(r[:d2])
            den.append(r[d2:d2 + 1])
        o = (num[0] / den[0] - num[1] * (lam / den[1])).T
        y = o * lax.rsqrt(jnp.mean(o * o, axis=-1, keepdims=True) + EPS) * dn_ref[...]
        y = y * (1.0 - lam_init) * _silu(z_ref[rows, :].astype(F32))
        o_ref[rows, :] = y.astype(o_ref.dtype)

    logits_stage(0, 0)
    if n_blocks % 2 == 0:
        def pair(p, carry):
            logits_stage(2 * p + 1, 1)
            softmax_stage(2 * p, 0)
            logits_stage(2 * p + 2, 0)
            softmax_stage(2 * p + 1, 1)
            return carry
        lax.fori_loop(0, n_blocks // 2 - 1, pair, 0)
        logits_stage(n_blocks - 1, 1)
        softmax_stage(n_blocks - 2, 0)
        softmax_stage(n_blocks - 1, 1)
    else:
        assert n_blocks == 1
        softmax_stage(0, 0)


def _attn_call(proj, rel_bias, lam_par, dnorm, lam_init, batch, seq, width, col0):
    m = proj.shape[0]
    nh = DIFF_HEADS
    hw = width // nh
    c0 = col0 // hw
    tq = min(ATTN_Q_ROWS, seq)
    blk = lambda c: pl.BlockSpec((seq, hw), lambda h, b: (b, c + h))
    return pl.pallas_call(
        functools.partial(_attn_kernel, lam_init),
        grid=(nh, batch),
        in_specs=[pl.BlockSpec(memory_space=pltpu.SMEM),
                  pl.BlockSpec(lam_par.shape, lambda h, b: (0, 0)),
                  pl.BlockSpec((1, hw), lambda h, b: (0, 0)),
                  blk(c0), blk(c0 + nh), blk(c0 + 2 * nh), blk(c0 + 3 * nh)],
        out_specs=pl.BlockSpec((seq, hw), lambda h, b: (b, h)),
        out_shape=jax.ShapeDtypeStruct((m, width), BF16),
        scratch_shapes=[pltpu.VMEM((2 * seq - tq, tq), F32),
                        pltpu.VMEM((2, 2, seq, tq), F32),
                        pltpu.VMEM((2, 2, 1, tq), F32),
                        pltpu.VMEM((hw + BF16_ROWS, seq), BF16)],
        compiler_params=_params("arbitrary", "arbitrary"),
        name="diff_attn",
    )(rel_bias, lam_par, dnorm, proj, proj, proj, proj)


def _merge_kernel(h_ref, za_ref, zb_ref, zc_ref, wm0_ref, wm1_ref, wm2_ref,
                  bm0_ref, bm1_ref, bm2_ref, wb0_ref, wb1_ref, wb2_ref, o_ref):
    hh = h_ref[...]
    acc = None
    for z_ref, wm_ref, bm_ref, wb_ref in ((za_ref, wm0_ref, bm0_ref, wb0_ref),
                                          (zb_ref, wm1_ref, bm1_ref, wb1_ref),
                                          (zc_ref, wm2_ref, bm2_ref, wb2_ref)):
        gate = _sigmoid(_dot(hh, wm_ref[...]) + bm_ref[...])
        term = gate * _dot(z_ref[...], wb_ref[0])
        acc = term if acc is None else acc + term
    o_ref[...] = acc.astype(o_ref.dtype)


def _merge_call(h, za, zb, zc, wm, bm, wb, layer):
    m, d = h.shape
    tm = min(512, m)
    tn = min(512, d)
    nj = d // tn
    row = pl.BlockSpec((tm, d), lambda j, i: (i, 0))
    wm_spec = lambda br: pl.BlockSpec((None, d, tn), lambda j, i: (layer, 0, br * nj + j))
    bm_spec = lambda br: pl.BlockSpec((None, 1, tn), lambda j, i: (layer, 0, br * nj + j))
    wb_spec = lambda br: pl.BlockSpec((None, 1, d, tn), lambda j, i: (layer, br, 0, j))
    return pl.pallas_call(
        _merge_kernel,
        grid=(nj, m // tm),
        in_specs=[row, row, row, row,
                  wm_spec(0), wm_spec(1), wm_spec(2),
                  bm_spec(0), bm_spec(1), bm_spec(2),
                  wb_spec(0), wb_spec(1), wb_spec(2)],
        out_specs=pl.BlockSpec((tm, tn), lambda j, i: (i, j)),
        out_shape=jax.ShapeDtypeStruct((m, d), BF16),
        compiler_params=_params("arbitrary", "arbitrary"),
        name="merge",
    )(h, za, zb, zc, wm, wm, wm, bm, bm, bm, wb, wb, wb)


def _out_kernel(mg_ref, w_ref, x_ref, g_ref, gn_ref, o_ref, *maybe_h_ref):
    out = _dot(mg_ref[...], w_ref[...])
    y = out * lax.rsqrt(jnp.mean(out * out, axis=-1, keepdims=True) + EPS) * g_ref[...]
    x = x_ref[...] + y
    o_ref[...] = x
    if maybe_h_ref:
        hn = x * lax.rsqrt(jnp.mean(x * x, axis=-1, keepdims=True) + EPS)
        maybe_h_ref[0][...] = (hn * gn_ref[...]).astype(BF16)


def _out_call(merged, w, layer, x2, g, g_next):
    m, d = x2.shape
    tm = min(512, m)
    row = pl.BlockSpec((tm, d), lambda i: (i, 0))
    vec = pl.BlockSpec((1, d), lambda i: (0, 0))
    with_h = g_next is not None
    out = pl.pallas_call(
        _out_kernel,
        grid=(m // tm,),
        in_specs=[row, pl.BlockSpec((None, d, d), lambda i: (layer, 0, 0)), row, vec, vec],
        out_specs=[row, row] if with_h else row,
        out_shape=([jax.ShapeDtypeStruct((m, d), F32), jax.ShapeDtypeStruct((m, d), BF16)]
                   if with_h else jax.ShapeDtypeStruct((m, d), F32)),
        compiler_params=_params("parallel"),
        name="out_proj",
    )(merged, w, x2, g.reshape(1, d), (g_next if with_h else g).reshape(1, d))
    return out if with_h else (out, None)


def kernel(x, norm_pre, w_in, gmlp_ln_g, gmlp_ln_b, gmlp_ws, gmlp_bs, gla_wa2, gla_ba, gla_norm,
           diff_lambda, diff_norm, rel_bias, w_branch, w_merge, b_merge, w_out, norm_post):
    batch, seq, d = x.shape
    depth = norm_pre.shape[0]
    width = d
    m = batch * seq
    hk = gla_wa2.shape[-1]
    lr_col = 3 * width + 2 * hk + 2 * width
    lr_w = 2 * GLA_RANK
    assert seq % GLA_CHUNK == 0 and seq % ATTN_Q_ROWS == 0 and seq % GMLP_CHUNK == 0
    assert w_in.shape[-1] == lr_col + lr_w + 4 * width

    cmats = jnp.asarray(_gla_constants(), BF16)
    w_t = jnp.swapaxes(w_in, 1, 2)
    ws_b = gmlp_ws.astype(BF16)
    bm = b_merge.reshape(depth, 1, -1)
    cast_plan = _mm_cast_plan(m, lr_col, w_merge, w_branch, w_out)
    if cast_plan is None:
        wm_b, wb_b, wo_b = w_merge.astype(BF16), w_branch.astype(BF16), w_out.astype(BF16)

    x2 = x.reshape(m, d)
    h = _rms_call(x2, norm_pre[0])
    for l in range(depth):
        wa = jnp.zeros((2, LANES, hk), F32)
        wa = wa.at[0, :GLA_RANK].set(gla_wa2[l, 0]).at[1, GLA_RANK:lr_w].set(gla_wa2[l, 1]).astype(BF16)
        ba = gla_ba[l].reshape(2, 1, hk)

        if l == 0 and cast_plan is not None:
            proj_ab, wm_b, wb_b, wo_b = _mm_cast_call(h, w_t, l, 0, lr_col, w_merge, w_branch, w_out,
                                                      cast_plan, "in_proj_ab_cast")
        else:
            proj_ab = _mm_call(h, w_t, l, 0, lr_col, "in_proj_ab")
        lr = _mm_call(h, w_t, l, lr_col, LANES, "lr_proj")
        gmlp_args = (proj_ab, gmlp_ln_g[l].reshape(1, width), gmlp_ln_b[l].reshape(1, width),
                     ws_b, gmlp_bs[l].T)
        if _mm_gmlp_rows(m, 4 * width):
            proj_c, za = _mm_gmlp_call(h, w_t, l, lr_col + lr_w, 4 * width, *gmlp_args, width,
                                       "in_proj_c_gmlp")
        else:
            proj_c = _mm_call(h, w_t, l, lr_col + lr_w, 4 * width, "in_proj_c")
            za = _gmlp_call(*gmlp_args, l, width)
        zb = _gla_call(proj_ab, lr, wa, ba, gla_norm[l].reshape(1, -1), cmats, batch, seq, width,
                       3 * width)
        lam_init = 0.8 - 0.6 * math.exp(-0.3 * l)
        zc = _attn_call(proj_c, rel_bias, diff_lambda[l], diff_norm[l].reshape(1, -1), lam_init,
                        batch, seq, width, 0)
        merged = _merge_call(h, za, zb, zc, wm_b, bm, wb_b, l)
        x2, h = _out_call(merged, wo_b, l, x2, norm_post[l],
                          norm_pre[l + 1] if l + 1 < depth else None)
    return x2.reshape(batch, seq, d)
```

```python
import functools
import math

import numpy as np
import jax
import jax.numpy as jnp
from jax import lax
from jax.experimental import pallas as pl
from jax.experimental.pallas import tpu as pltpu

F32 = jnp.float32
BF16 = jnp.bfloat16

EPS = 1e-6
LOG2E = math.log2(math.e)
N_BRANCH = 3
GMLP_CHUNK = 128
GMLP_GROUPS = 8
GLA_HEADS = 4
GLA_RANK = 16
GLA_TAU = 16.0
GLA_SUB = 64
GLA_CHUNK = 2 * GLA_SUB
GLA_GROUP = 8
DIFF_HEADS = 8
ATTN_Q_ROWS = 256
ATTN_BIAS_ROWS = 32
REL_BUCKETS = 32
T5_STARTS = (1, 2, 3, 4, 5, 6, 7, 8, 12, 16, 23, 32, 46, 64, 91)

LANES = 128
SUBLANES = 8
BF16_ROWS = 2 * SUBLANES
VMEM_LIMIT = 52 * 1024 * 1024


def _params(*sem):
    return pltpu.CompilerParams(dimension_semantics=sem, vmem_limit_bytes=VMEM_LIMIT)


GELU_C = math.sqrt(2.0 / math.pi)


def _gelu(x):
    hx = 0.5 * x
    return hx + hx * jnp.tanh(x * (GELU_C + (GELU_C * 0.044715) * (x * x)))


def _sigmoid(x):
    return 0.5 + 0.5 * jnp.tanh(0.5 * x)


def _silu(x):
    hx = 0.5 * x
    return hx + hx * jnp.tanh(hx)


def _log2_sigmoid(x):
    u = x * LOG2E
    return jnp.minimum(u, 0.0) - jnp.log2(1.0 + jnp.exp2(-jnp.abs(u)))


def _dot(a, b):
    return jnp.dot(a, b, preferred_element_type=F32)


def _dot_nt(a, b):
    return lax.dot_general(a, b, (((1,), (1,)), ((), ())), preferred_element_type=F32)


def _rms_kernel(x_ref, g_ref, o_ref):
    x = x_ref[...]
    y = x * lax.rsqrt(jnp.mean(x * x, axis=-1, keepdims=True) + EPS)
    o_ref[...] = (y * g_ref[...]).astype(o_ref.dtype)


def _rms_call(x2, g):
    m, d = x2.shape
    tm = min(512, m)
    return pl.pallas_call(
        _rms_kernel,
        grid=(m // tm,),
        in_specs=[pl.BlockSpec((tm, d), lambda i: (i, 0)),
                  pl.BlockSpec((1, d), lambda i: (0, 0))],
        out_specs=pl.BlockSpec((tm, d), lambda i: (i, 0)),
        out_shape=jax.ShapeDtypeStruct((m, d), BF16),
        compiler_params=_params("parallel"),
        name="rms_pre",
    )(x2, g.reshape(1, d))


def _mm_kernel(a_ref, w_ref, o_ref, w_scr):
    @pl.when(pl.program_id(1) == 0)
    def _():
        w_scr[...] = w_ref[0].astype(BF16)
    o_ref[...] = _dot_nt(a_ref[...], w_scr[...]).astype(o_ref.dtype)


def _mm_call(a, w_t, layer, row0, n, name):
    m, k = a.shape
    tm = min(1024, m)
    tn = min(1024, n)
    return pl.pallas_call(
        _mm_kernel,
        grid=(n // tn, m // tm),
        in_specs=[pl.BlockSpec((tm, k), lambda j, i: (i, 0)),
                  pl.BlockSpec((pl.Element(1), pl.Element(tn), pl.Element(k)),
                               lambda j, i: (layer, pl.multiple_of(row0 + j * tn, SUBLANES), 0))],
        out_specs=pl.BlockSpec((tm, tn), lambda j, i: (i, j)),
        out_shape=jax.ShapeDtypeStruct((m, n), BF16),
        scratch_shapes=[pltpu.VMEM((tn, k), BF16)],
        compiler_params=_params("arbitrary", "arbitrary"),
        name=name,
    )(a, w_t)


def _mm_cast_kernel(a_ref, w_ref, wm_ref, wb_ref, wo_ref, o_ref, wm_o, wb_o, wo_o, w_scr):
    _mm_kernel(a_ref, w_ref, o_ref, w_scr)
    wm_o[...] = wm_ref[...].astype(BF16)
    wb_o[...] = wb_ref[...].astype(BF16)
    wo_o[...] = wo_ref[...].astype(BF16)


def _mm_cast_plan(m, n, wm, wb, wo):
    steps = (n // min(1024, n)) * (m // min(1024, m))
    if steps % N_BRANCH:
        return None
    third = steps // N_BRANCH
    rows_m, rows_b, rows_o = wm.shape[0] * wm.shape[1], math.prod(wb.shape[:3]), wo.shape[0] * wo.shape[1]
    if rows_m % third or rows_b % steps or rows_o % third:
        return None
    plan = (rows_m // third, rows_b // steps, rows_o // third)
    return plan if all(r % BF16_ROWS == 0 for r in plan) else None


def _mm_cast_call(a, w_t, layer, row0, n, wm, wb, wo, plan, name):
    m, k = a.shape
    d = wo.shape[-1]
    tm = min(1024, m)
    tn = min(1024, n)
    ni = m // tm
    steps = (n // tn) * ni
    third = steps // N_BRANCH
    rm, rb, ro = plan
    wm2, wb2, wo2 = wm.reshape(-1, N_BRANCH * d), wb.reshape(-1, d), wo.reshape(-1, d)
    step = lambda j, i: j * ni + i
    cast_specs = [pl.BlockSpec((rm, d), lambda j, i: (step(j, i) % third, step(j, i) // third)),
                  pl.BlockSpec((rb, d), lambda j, i: (step(j, i), 0)),
                  pl.BlockSpec((ro, d), lambda j, i: (jnp.minimum(step(j, i), third - 1), 0))]
    out = pl.pallas_call(
        _mm_cast_kernel,
        grid=(n // tn, ni),
        in_specs=[pl.BlockSpec((tm, k), lambda j, i: (i, 0)),
                  pl.BlockSpec((pl.Element(1), pl.Element(tn), pl.Element(k)),
                               lambda j, i: (layer, pl.multiple_of(row0 + j * tn, SUBLANES), 0))]
                 + cast_specs,
        out_specs=[pl.BlockSpec((tm, tn), lambda j, i: (i, j))] + cast_specs,
        out_shape=[jax.ShapeDtypeStruct((m, n), BF16)]
                  + [jax.ShapeDtypeStruct(w.shape, BF16) for w in (wm2, wb2, wo2)],
        scratch_shapes=[pltpu.VMEM((tn, k), BF16)],
        compiler_params=_params("arbitrary", "arbitrary"),
        name=name,
    )(a, w_t, wm2, wb2, wo2)
    return out[0], out[1].reshape(wm.shape), out[2].reshape(wb.shape), out[3].reshape(wo.shape)


def _gmlp_kernel(u_ref, v_ref, z_ref, lng_ref, lnb_ref, ws_ref, bs_ref, o_ref):
    tm, w = v_ref.shape
    gc = w // GMLP_GROUPS
    gv = _gelu(v_ref[...].astype(F32))
    mu = jnp.mean(gv, axis=-1, keepdims=True)
    cen = gv - mu
    var = jnp.mean(cen * cen, axis=-1, keepdims=True)
    sv = (cen * lax.rsqrt(var + EPS) * lng_ref[...] + lnb_ref[...]).astype(BF16)
    for c in range(tm // GMLP_CHUNK):
        rows = slice(c * GMLP_CHUNK, (c + 1) * GMLP_CHUNK)
        for g in range(GMLP_GROUPS):
            cols = slice(g * gc, (g + 1) * gc)
            mixed = _dot(ws_ref[g], sv[rows, cols]) + bs_ref[:, g:g + 1]
            u = _gelu(u_ref[rows, cols].astype(F32))
            z = _silu(z_ref[rows, cols].astype(F32))
            o_ref[rows, cols] = (u * mixed * z).astype(o_ref.dtype)


def _gmlp_call(proj, lng, lnb, ws, bs_t, layer, width):
    m = proj.shape[0]
    tm = min(256, m)
    blk = lambda c: pl.BlockSpec((tm, width), lambda i: (i, c))
    full = lambda a: pl.BlockSpec(a.shape, lambda i: (0,) * a.ndim)
    return pl.pallas_call(
        _gmlp_kernel,
        grid=(m // tm,),
        in_specs=[blk(0), blk(1), blk(2), full(lng), full(lnb),
                  pl.BlockSpec((None,) + ws.shape[1:], lambda i: (layer, 0, 0, 0)), full(bs_t)],
        out_specs=pl.BlockSpec((tm, width), lambda i: (i, 0)),
        out_shape=jax.ShapeDtypeStruct((m, width), BF16),
        compiler_params=_params("parallel"),
        name="gmlp",
    )(proj, proj, proj, lng, lnb, ws, bs_t)


def _mm_gmlp_kernel(a_ref, w_ref, u_ref, v_ref, z_ref, lng_ref, lnb_ref, ws_ref, bs_ref,
                    o_ref, za_ref, w_scr):
    _mm_kernel(a_ref, w_ref, o_ref, w_scr)
    _gmlp_kernel(u_ref, v_ref, z_ref, lng_ref, lnb_ref, ws_ref, bs_ref, za_ref)


def _mm_gmlp_rows(m, n):
    steps = (n // min(1024, n)) * (m // min(1024, m))
    return m // steps if m % (steps * GMLP_CHUNK) == 0 else 0


def _mm_gmlp_call(a, w_t, layer, row0, n, proj, lng, lnb, ws, bs_t, width, name):
    m, k = a.shape
    tm = min(1024, m)
    tn = min(1024, n)
    ni = m // tm
    gr = _mm_gmlp_rows(m, n)
    slab = lambda c: pl.BlockSpec((gr, width), lambda j, i: (j * ni + i, c))
    full = lambda x: pl.BlockSpec(x.shape, lambda j, i: (0,) * x.ndim)
    return pl.pallas_call(
        _mm_gmlp_kernel,
        grid=(n // tn, ni),
        in_specs=[pl.BlockSpec((tm, k), lambda j, i: (i, 0)),
                  pl.BlockSpec((pl.Element(1), pl.Element(tn), pl.Element(k)),
                               lambda j, i: (layer, pl.multiple_of(row0 + j * tn, SUBLANES), 0)),
                  slab(0), slab(1), slab(2), full(lng), full(lnb),
                  pl.BlockSpec((None,) + ws.shape[1:], lambda j, i: (layer, 0, 0, 0)), full(bs_t)],
        out_specs=[pl.BlockSpec((tm, tn), lambda j, i: (i, j)), slab(0)],
        out_shape=[jax.ShapeDtypeStruct((m, n), BF16), jax.ShapeDtypeStruct((m, width), BF16)],
        scratch_shapes=[pltpu.VMEM((tn, k), BF16)],
        compiler_params=_params("arbitrary", "arbitrary"),
        name=name,
    )(a, w_t, proj, proj, proj, lng, lnb, ws, bs_t)


def _gla_constants():
    c, s = GLA_CHUNK, GLA_SUB
    i = np.arange(c)[:, None]
    j = np.arange(c)[None, :]
    same = (i // s) == (j // s)
    mats = []
    for rev in (False, True):
        if not rev:
            cum = same & (j <= i)
            ref = same & ((j % s) <= s // 2 - 1)
        else:
            cum = same & (j >= i)
            ref = same & ((j % s) >= s // 2)
        cum, ref, tot = (t.astype(np.float32) for t in (cum, ref, same))
        blocks = np.concatenate([cum, cum - ref, ref - cum, tot - cum], axis=0)
        mats.append(np.concatenate([blocks, blocks], axis=1))
    return np.stack(mats)


def _gla_kernel(q_ref, k_ref, v_ref, z_ref, lr_ref, wa_ref, ba_ref, gn_ref, cm_ref, o_ref,
                vt_scr, st_scr, o_scr):
    s_len, dk = q_ref.shape
    c, sub = GLA_CHUNK, GLA_SUB
    n_chunks = s_len // c

    vt_scr[...] = v_ref[...].T
    st_scr[...] = jnp.zeros_like(st_scr)

    ri = lax.broadcasted_iota(jnp.int32, (c, c), 0)
    ci = lax.broadcasted_iota(jnp.int32, (c, c), 1)
    same = (ri // sub) == (ci // sub)
    row_in_sub0 = lax.broadcasted_iota(jnp.int32, (c, 1), 0) < sub

    n_grp = min(GLA_GROUP, n_chunks)
    assert n_chunks % n_grp == 0

    def group(d, gi, first):
        rev = d == 1
        cs = range(n_grp)
        base = pl.multiple_of(gi * (n_grp * c), n_grp * c)
        rows = pl.ds(base, n_grp * c)
        crow = [pl.ds(pl.multiple_of(base + i * c, c), c) for i in cs]
        part = lambda x, i: x[i * c:(i + 1) * c]
        q = q_ref[rows, :].astype(F32) * (dk ** -0.5)
        k = k_ref[rows, :].astype(F32)
        g = _log2_sigmoid(_dot(lr_ref[rows, :], wa_ref[d]) + ba_ref[d]) * (1.0 / GLA_TAU)
        g_hi = g.astype(BF16)
        g_lo = (g - g_hi.astype(F32)).astype(BF16)
        e = [_dot(cm_ref[d], jnp.concatenate([part(g_hi, i), part(g_lo, i)], axis=0)) for i in cs]
        cum = [e[i][:c] for i in cs]
        if rev:
            last0, last1 = [x[0:1] for x in cum], [x[sub:sub + 1] for x in cum]
            diag_mask = same & (ci >= ri)
            off_mask = (ri < sub) & (ci >= sub)
        else:
            last0, last1 = [x[sub - 1:sub] for x in cum], [x[c - 1:c] for x in cum]
            diag_mask = same & (ci <= ri)
            off_mask = (ri >= sub) & (ci < sub)
        q_in = [(part(q, i) * jnp.exp2(e[i][c:2 * c])).astype(BF16) for i in cs]
        k_in = [(part(k, i) * jnp.exp2(e[i][2 * c:3 * c])).astype(BF16) for i in cs]
        q_dec = [part(q, i) * jnp.exp2(cum[i]) for i in cs]
        k_dec = [part(k, i) * jnp.exp2(e[i][3 * c:]) for i in cs]
        s_diag = [_dot_nt(q_in[i], k_in[i]) for i in cs]
        s_off = [_dot_nt(q_dec[i].astype(BF16), k_dec[i].astype(BF16)) for i in cs]
        scores = [(jnp.where(diag_mask, s_diag[i], 0.0)
                   + jnp.where(off_mask, s_off[i], 0.0)).astype(BF16) for i in cs]
        o_intra = [_dot(scores[i], v_ref[crow[i], :]) for i in cs]
        if rev:
            q_st = [q_dec[i] * jnp.where(row_in_sub0, jnp.exp2(last1[i]), 1.0) for i in cs]
            k_st = [k_dec[i] * jnp.where(row_in_sub0, 1.0, jnp.exp2(last0[i])) for i in cs]
        else:
            q_st = [q_dec[i] * jnp.where(row_in_sub0, 1.0, jnp.exp2(last0[i])) for i in cs]
            k_st = [k_dec[i] * jnp.where(row_in_sub0, jnp.exp2(last1[i]), 1.0) for i in cs]
        kv_t = [_dot(vt_scr[:, crow[i]], k_st[i].astype(BF16)) for i in cs]
        st = st_scr[d]
        seen = [None] * n_grp
        for i in (reversed(cs) if rev else cs):
            seen[i] = st.astype(BF16)
            st = st * jnp.exp2(last0[i] + last1[i]) + kv_t[i]
        st_scr[d] = st
        for i in cs:
            o = o_intra[i] + _dot_nt(q_st[i].astype(BF16), seen[i])
            if first:
                o_scr[crow[i], :] = o
            else:
                o_scr[crow[i], :] += o

    n_groups = n_chunks // n_grp
    half = n_groups // 2

    def sweep(first):
        def body(t, carry):
            group(0, t, first)
            group(1, n_groups - 1 - t, first)
            return carry
        return body

    if n_groups == 1:
        group(0, 0, True)
        group(1, 0, False)
    elif n_groups % 2 == 0:
        lax.fori_loop(0, half, sweep(True), 0)
        lax.fori_loop(half, n_groups, sweep(False), 0)
    else:
        o_scr[...] = jnp.zeros_like(o_scr)
        lax.fori_loop(0, n_groups, sweep(False), 0)

    o = o_scr[...]
    y = o * lax.rsqrt(jnp.mean(o * o, axis=-1, keepdims=True) + EPS) * gn_ref[...]
    o_ref[...] = (y * _silu(z_ref[...].astype(F32))).astype(o_ref.dtype)


def _gla_call(proj, lr, wa, ba, gnorm, cmats, batch, seq, width, col0):
    m = proj.shape[0]
    h = GLA_HEADS
    dv = width // h
    dk = dv // 2
    qb, kb = col0 // dk, col0 // dk + h
    vb, zb = (col0 + 2 * h * dk) // dv, (col0 + 2 * h * dk) // dv + h
    return pl.pallas_call(
        _gla_kernel,
        grid=(batch, h),
        in_specs=[pl.BlockSpec((seq, dk), lambda b, i: (b, qb + i)),
                  pl.BlockSpec((seq, dk), lambda b, i: (b, kb + i)),
                  pl.BlockSpec((seq, dv), lambda b, i: (b, vb + i)),
                  pl.BlockSpec((seq, dv), lambda b, i: (b, zb + i)),
                  pl.BlockSpec((seq, LANES), lambda b, i: (b, 0)),
                  pl.BlockSpec((2, LANES, dk), lambda b, i: (0, 0, i)),
                  pl.BlockSpec((2, 1, dk), lambda b, i: (0, 0, i)),
                  pl.BlockSpec((1, dv), lambda b, i: (0, 0)),
                  pl.BlockSpec(cmats.shape, lambda b, i: (0, 0, 0))],
        out_specs=pl.BlockSpec((seq, dv), lambda b, i: (b, i)),
        out_shape=jax.ShapeDtypeStruct((m, width), BF16),
        scratch_shapes=[pltpu.VMEM((dv, seq), BF16),
                        pltpu.VMEM((2, dv, dk), F32),
                        pltpu.VMEM((seq, dv), F32)],
        compiler_params=_params("parallel", "parallel"),
        name="gla",
    )(proj, proj, proj, proj, lr, wa, ba, gnorm, cmats)


def _attn_kernel(lam_init, relb_ref, lam_ref, dn_ref, q_ref, k_ref, v_ref, z_ref, o_ref,
                 bias_scr, s_scr, m_scr, vt_scr):
    h = pl.program_id(0)
    s_len, d2 = q_ref.shape
    d = d2 // 2
    tq = bias_scr.shape[1]
    bt = ATTN_BIAS_ROWS
    n_tiles = bias_scr.shape[0] // bt

    @pl.when(pl.program_id(1) == 0)
    def _():
        def tile(t, carry):
            c0 = pl.multiple_of(t * bt, bt)
            rel = (lax.broadcasted_iota(jnp.int32, (bt, tq), 0) + (c0 - (s_len - tq))
                   - lax.broadcasted_iota(jnp.int32, (bt, tq), 1))
            dist = jnp.abs(rel)
            neg = jnp.full((bt, tq), relb_ref[0, h], F32)
            pos = jnp.full((bt, tq), relb_ref[REL_BUCKETS // 2, h], F32)
            for bucket, start in enumerate(T5_STARTS, start=1):
                far = dist >= start
                neg = jnp.where(far, relb_ref[bucket, h], neg)
                pos = jnp.where(far, relb_ref[REL_BUCKETS // 2 + bucket, h], pos)
            bias_scr[pl.ds(c0, bt), :] = jnp.where(rel > 0, pos, neg) * LOG2E
            return carry
        n_buckets = len(T5_STARTS)
        reach = T5_STARTS[-1]
        band_lo = max(0, (s_len - tq - reach) // bt)
        band_hi = min(n_tiles, -(-(s_len + reach) // bt))
        if band_lo > 0:
            bias_scr[:band_lo * bt, :] = jnp.full((band_lo * bt, tq), relb_ref[n_buckets, h] * LOG2E, F32)
        if band_hi < n_tiles:
            bias_scr[band_hi * bt:, :] = jnp.full(
                ((n_tiles - band_hi) * bt, tq), relb_ref[REL_BUCKETS // 2 + n_buckets, h] * LOG2E, F32)
        lax.fori_loop(band_lo, band_hi, tile, 0)

    vt_scr[:d2, :] = v_ref[...].T
    vt_scr[d2:, :] = jnp.ones((vt_scr.shape[0] - d2, s_len), BF16)

    lv = lam_ref[...]
    lam = (jnp.exp(jnp.sum(lv[0:1] * lv[1:2], axis=-1, keepdims=True))
           - jnp.exp(jnp.sum(lv[2:3] * lv[3:4], axis=-1, keepdims=True)) + lam_init)

    n_blocks = s_len // tq

    def logits_stage(i, slot):
        r0 = pl.multiple_of(i * tq, tq)
        bias = bias_scr[pl.ds(pl.multiple_of(s_len - tq - r0, tq), s_len), :]
        q = (q_ref[pl.ds(r0, tq), :].astype(F32) * (d ** -0.5 * LOG2E)).astype(BF16)
        for m in range(2):
            t = _dot_nt(k_ref[:, m * d:(m + 1) * d], q[:, m * d:(m + 1) * d]) + bias
            s_scr[slot, m] = t
            m_scr[slot, m] = jnp.max(t, axis=0, keepdims=True)

    def softmax_stage(i, slot):
        rows = pl.ds(pl.multiple_of(i * tq, tq), tq)
        num, den = [], []
        for m in range(2):
            p = jnp.exp2(s_scr[slot, m] - m_scr[slot, m]).astype(BF16)
            r = _dot(vt_scr[...], p)
            num.append(r[:d2])
            den.append(r[d2:d2 + 1])
        o = (num[0] / den[0] - num[1] * (lam / den[1])).T
        y = o * lax.rsqrt(jnp.mean(o * o, axis=-1, keepdims=True) + EPS) * dn_ref[...]
        y = y * (1.0 - lam_init) * _silu(z_ref[rows, :].astype(F32))
        o_ref[rows, :] = y.astype(o_ref.dtype)

    logits_stage(0, 0)
    if n_blocks % 2 == 0:
        def pair(p, carry):
            logits_stage(2 * p + 1, 1)
            softmax_stage(2 * p, 0)
            logits_stage(2 * p + 2, 0)
            softmax_stage(2 * p + 1, 1)
            return carry
        lax.fori_loop(0, n_blocks // 2 - 1, pair, 0)
        logits_stage(n_blocks - 1, 1)
        softmax_stage(n_blocks - 2, 0)
        softmax_stage(n_blocks - 1, 1)
    else:
        assert n_blocks == 1
        softmax_stage(0, 0)


def _attn_call(proj, rel_bias, lam_par, dnorm, lam_init, batch, seq, width, col0):
    m = proj.shape[0]
    nh = DIFF_HEADS
    hw = width // nh
    c0 = col0 // hw
    tq = min(ATTN_Q_ROWS, seq)
    blk = lambda c: pl.BlockSpec((seq, hw), lambda h, b: (b, c + h))
    return pl.pallas_call(
        functools.partial(_attn_kernel, lam_init),
        grid=(nh, batch),
        in_specs=[pl.BlockSpec(memory_space=pltpu.SMEM),
                  pl.BlockSpec(lam_par.shape, lambda h, b: (0, 0)),
                  pl.BlockSpec((1, hw), lambda h, b: (0, 0)),
                  blk(c0), blk(c0 + nh), blk(c0 + 2 * nh), blk(c0 + 3 * nh)],
        out_specs=pl.BlockSpec((seq, hw), lambda h, b: (b, h)),
        out_shape=jax.ShapeDtypeStruct((m, width), BF16),
        scratch_shapes=[pltpu.VMEM((2 * seq - tq, tq), F32),
                        pltpu.VMEM((2, 2, seq, tq), F32),
                        pltpu.VMEM((2, 2, 1, tq), F32),
                        pltpu.VMEM((hw + BF16_ROWS, seq), BF16)],
        compiler_params=_params("arbitrary", "arbitrary"),
        name="diff_attn",
    )(rel_bias, lam_par, dnorm, proj, proj, proj, proj)


def _merge_kernel(h_ref, za_ref, zb_ref, zc_ref, wm0_ref, wm1_ref, wm2_ref,
                  bm0_ref, bm1_ref, bm2_ref, wb0_ref, wb1_ref, wb2_ref, o_ref):
    hh = h_ref[...]
    acc = None
    for z_ref, wm_ref, bm_ref, wb_ref in ((za_ref, wm0_ref, bm0_ref, wb0_ref),
                                          (zb_ref, wm1_ref, bm1_ref, wb1_ref),
                                          (zc_ref, wm2_ref, bm2_ref, wb2_ref)):
        gate = _sigmoid(_dot(hh, wm_ref[...]) + bm_ref[...])
        term = gate * _dot(z_ref[...], wb_ref[0])
        acc = term if acc is None else acc + term
    o_ref[...] = acc.astype(o_ref.dtype)


def _merge_call(h, za, zb, zc, wm, bm, wb, layer):
    m, d = h.shape
    tm = min(512, m)
    tn = min(512, d)
    nj = d // tn
    row = pl.BlockSpec((tm, d), lambda j, i: (i, 0))
    wm_spec = lambda br: pl.BlockSpec((None, d, tn), lambda j, i: (layer, 0, br * nj + j))
    bm_spec = lambda br: pl.BlockSpec((None, 1, tn), lambda j, i: (layer, 0, br * nj + j))
    wb_spec = lambda br: pl.BlockSpec((None, 1, d, tn), lambda j, i: (layer, br, 0, j))
    return pl.pallas_call(
        _merge_kernel,
        grid=(nj, m // tm),
        in_specs=[row, row, row, row,
                  wm_spec(0), wm_spec(1), wm_spec(2),
                  bm_spec(0), bm_spec(1), bm_spec(2),
                  wb_spec(0), wb_spec(1), wb_spec(2)],
        out_specs=pl.BlockSpec((tm, tn), lambda j, i: (i, j)),
        out_shape=jax.ShapeDtypeStruct((m, d), BF16),
        compiler_params=_params("arbitrary", "arbitrary"),
        name="merge",
    )(h, za, zb, zc, wm, wm, wm, bm, bm, bm, wb, wb, wb)


def _out_kernel(mg_ref, w_ref, x_ref, g_ref, gn_ref, o_ref, *maybe_h_ref):
    out = _dot(mg_ref[...], w_ref[...])
    y = out * lax.rsqrt(jnp.mean(out * out, axis=-1, keepdims=True) + EPS) * g_ref[...]
    x = x_ref[...] + y
    o_ref[...] = x
    if maybe_h_ref:
        hn = x * lax.rsqrt(jnp.mean(x * x, axis=-1, keepdims=True) + EPS)
        maybe_h_ref[0][...] = (hn * gn_ref[...]).astype(BF16)


def _out_call(merged, w, layer, x2, g, g_next):
    m, d = x2.shape
    tm = min(512, m)
    row = pl.BlockSpec((tm, d), lambda i: (i, 0))
    vec = pl.BlockSpec((1, d), lambda i: (0, 0))
    with_h = g_next is not None
    out = pl.pallas_call(
        _out_kernel,
        grid=(m // tm,),
        in_specs=[row, pl.BlockSpec((None, d, d), lambda i: (layer, 0, 0)), row, vec, vec],
        out_specs=[row, row] if with_h else row,
        out_shape=([jax.ShapeDtypeStruct((m, d), F32), jax.ShapeDtypeStruct((m, d), BF16)]
                   if with_h else jax.ShapeDtypeStruct((m, d), F32)),
        compiler_params=_params("parallel"),
        name="out_proj",
    )(merged, w, x2, g.reshape(1, d), (g_next if with_h else g).reshape(1, d))
    return out if with_h else (out, None)


def kernel(x, norm_pre, w_in, gmlp_ln_g, gmlp_ln_b, gmlp_ws, gmlp_bs, gla_wa2, gla_ba, gla_norm,
           diff_lambda, diff_norm, rel_bias, w_branch, w_merge, b_merge, w_out, norm_post):
    batch, seq, d = x.shape
    depth = norm_pre.shape[0]
    width = d
    m = batch * seq
    hk = gla_wa2.shape[-1]
    lr_col = 3 * width + 2 * hk + 2 * width
    lr_w = 2 * GLA_RANK
    assert seq % GLA_CHUNK == 0 and seq % ATTN_Q_ROWS == 0 and seq % GMLP_CHUNK == 0
    assert w_in.shape[-1] == lr_col + lr_w + 4 * width

    cmats = jnp.asarray(_gla_constants(), BF16)
    w_t = jnp.swapaxes(w_in, 1, 2)
    ws_b = gmlp_ws.astype(BF16)
    bm = b_merge.reshape(depth, 1, -1)
    cast_plan = _mm_cast_plan(m, lr_col, w_merge, w_branch, w_out)
    if cast_plan is None:
        wm_b, wb_b, wo_b = w_merge.astype(BF16), w_branch.astype(BF16), w_out.astype(BF16)

    x2 = x.reshape(m, d)
    h = _rms_call(x2, norm_pre[0])
    for l in range(depth):
        wa = jnp.zeros((2, LANES, hk), F32)
        wa = wa.at[0, :GLA_RANK].set(gla_wa2[l, 0]).at[1, GLA_RANK:lr_w].set(gla_wa2[l, 1]).astype(BF16)
        ba = gla_ba[l].reshape(2, 1, hk)

        if l == 0 and cast_plan is not None:
            proj_ab, wm_b, wb_b, wo_b = _mm_cast_call(h, w_t, l, 0, lr_col, w_merge, w_branch, w_out,
                                                      cast_plan, "in_proj_ab_cast")
        else:
            proj_ab = _mm_call(h, w_t, l, 0, lr_col, "in_proj_ab")
        lr = _mm_call(h, w_t, l, lr_col, LANES, "lr_proj")
        gmlp_args = (proj_ab, gmlp_ln_g[l].reshape(1, width), gmlp_ln_b[l].reshape(1, width),
                     ws_b, gmlp_bs[l].T)
        if _mm_gmlp_rows(m, 4 * width):
            proj_c, za = _mm_gmlp_call(h, w_t, l, lr_col + lr_w, 4 * width, *gmlp_args, width,
                                       "in_proj_c_gmlp")
        else:
            proj_c = _mm_call(h, w_t, l, lr_col + lr_w, 4 * width, "in_proj_c")
            za = _gmlp_call(*gmlp_args, l, width)
        zb = _gla_call(proj_ab, lr, wa, ba, gla_norm[l].reshape(1, -1), cmats, batch, seq, width,
                       3 * width)
        lam_init = 0.8 - 0.6 * math.exp(-0.3 * l)
        zc = _attn_call(proj_c, rel_bias, diff_lambda[l], diff_norm[l].reshape(1, -1), lam_init,
                        batch, seq, width, 0)
        merged = _merge_call(h, za, zb, zc, wm_b, bm, wb_b, l)
        x2, h = _out_call(merged, wo_b, l, x2, norm_post[l],
                          norm_pre[l + 1] if l + 1 < depth else None)
    return x2.reshape(batch, seq, d)
```

```python
import functools
import math

import numpy as np
import jax
import jax.numpy as jnp
from jax import lax
from jax.experimental import pallas as pl
from jax.experimental.pallas import tpu as pltpu

F32 = jnp.float32
BF16 = jnp.bfloat16

EPS = 1e-6
LOG2E = math.log2(math.e)
N_BRANCH = 3
GMLP_CHUNK = 128
GMLP_GROUPS = 8
GLA_HEADS = 4
GLA_RANK = 16
GLA_TAU = 16.0
GLA_SUB = 64
GLA_CHUNK = 2 * GLA_SUB
GLA_GROUP = 8
DIFF_HEADS = 8
ATTN_Q_ROWS = 256
ATTN_BIAS_ROWS = 32
REL_BUCKETS = 32
T5_STARTS = (1, 2, 3, 4, 5, 6, 7, 8, 12, 16, 23, 32, 46, 64, 91)

LANES = 128
SUBLANES = 8
BF16_ROWS = 2 * SUBLANES
VMEM_LIMIT = 52 * 1024 * 1024
MM_TILE = 1024
ROW_TILE = 512
MERGE_COLS = 512
GMLP_ROWS = 256


def _params(*sem):
    return pltpu.CompilerParams(dimension_semantics=sem, vmem_limit_bytes=VMEM_LIMIT)


GELU_C = math.sqrt(2.0 / math.pi)


def _gelu(x):
    hx = 0.5 * x
    return hx + hx * jnp.tanh(x * (GELU_C + (GELU_C * 0.044715) * (x * x)))


def _sigmoid(x):
    return 0.5 + 0.5 * jnp.tanh(0.5 * x)


def _silu(x):
    hx = 0.5 * x
    return hx + hx * jnp.tanh(hx)


def _log2_sigmoid(x):
    u = x * LOG2E
    return jnp.minimum(u, 0.0) - jnp.log2(1.0 + jnp.exp2(-jnp.abs(u)))


def _dot(a, b):
    return jnp.dot(a, b, preferred_element_type=F32)


def _dot_nt(a, b):
    return lax.dot_general(a, b, (((1,), (1,)), ((), ())), preferred_element_type=F32)


def _rms_kernel(x_ref, g_ref, o_ref):
    x = x_ref[...]
    y = x * lax.rsqrt(jnp.mean(x * x, axis=-1, keepdims=True) + EPS)
    o_ref[...] = (y * g_ref[...]).astype(o_ref.dtype)


def _rms_call(x2, g):
    m, d = x2.shape
    tm = min(ROW_TILE, m)
    return pl.pallas_call(
        _rms_kernel,
        grid=(m // tm,),
        in_specs=[pl.BlockSpec((tm, d), lambda i: (i, 0)),
                  pl.BlockSpec((1, d), lambda i: (0, 0))],
        out_specs=pl.BlockSpec((tm, d), lambda i: (i, 0)),
        out_shape=jax.ShapeDtypeStruct((m, d), BF16),
        compiler_params=_params("parallel"),
        name="rms_pre",
    )(x2, g.reshape(1, d))


def _mm_kernel(a_ref, w_ref, o_ref, w_scr):
    @pl.when(pl.program_id(1) == 0)
    def _():
        w_scr[...] = w_ref[0].astype(BF16)
    o_ref[...] = _dot_nt(a_ref[...], w_scr[...]).astype(o_ref.dtype)


def _mm_tiles(m, n):
    return min(MM_TILE, m), min(MM_TILE, n)


def _mm_call(a, w_t, layer, row0, n, name):
    m, k = a.shape
    tm, tn = _mm_tiles(m, n)
    return pl.pallas_call(
        _mm_kernel,
        grid=(n // tn, m // tm),
        in_specs=[pl.BlockSpec((tm, k), lambda j, i: (i, 0)),
                  pl.BlockSpec((pl.Element(1), pl.Element(tn), pl.Element(k)),
                               lambda j, i: (layer, pl.multiple_of(row0 + j * tn, SUBLANES), 0))],
        out_specs=pl.BlockSpec((tm, tn), lambda j, i: (i, j)),
        out_shape=jax.ShapeDtypeStruct((m, n), BF16),
        scratch_shapes=[pltpu.VMEM((tn, k), BF16)],
        compiler_params=_params("arbitrary", "arbitrary"),
        name=name,
    )(a, w_t)


def _mm_cast_kernel(a_ref, w_ref, wm_ref, wb_ref, wo_ref, o_ref, wm_o, wb_o, wo_o, w_scr):
    _mm_kernel(a_ref, w_ref, o_ref, w_scr)
    wm_o[...] = wm_ref[...].astype(BF16)
    wb_o[...] = wb_ref[...].astype(BF16)
    wo_o[...] = wo_ref[...].astype(BF16)


def _mm_cast_plan(m, n, wm, wb, wo):
    tm, tn = _mm_tiles(m, n)
    steps = (n // tn) * (m // tm)
    if steps % N_BRANCH:
        return None
    third = steps // N_BRANCH
    rows_m, rows_b, rows_o = wm.shape[0] * wm.shape[1], math.prod(wb.shape[:3]), wo.shape[0] * wo.shape[1]
    if rows_m % third or rows_b % steps or rows_o % third:
        return None
    plan = (rows_m // third, rows_b // steps, rows_o // third)
    return plan if all(r % BF16_ROWS == 0 for r in plan) else None


def _mm_cast_call(a, w_t, layer, row0, n, wm, wb, wo, plan, name):
    m, k = a.shape
    d = wo.shape[-1]
    tm, tn = _mm_tiles(m, n)
    ni = m // tm
    steps = (n // tn) * ni
    third = steps // N_BRANCH
    rm, rb, ro = plan
    wm2, wb2, wo2 = wm.reshape(-1, N_BRANCH * d), wb.reshape(-1, d), wo.reshape(-1, d)
    step = lambda j, i: j * ni + i
    cast_specs = [pl.BlockSpec((rm, d), lambda j, i: (step(j, i) % third, step(j, i) // third)),
                  pl.BlockSpec((rb, d), lambda j, i: (step(j, i), 0)),
                  pl.BlockSpec((ro, d), lambda j, i: (jnp.minimum(step(j, i), third - 1), 0))]
    out = pl.pallas_call(
        _mm_cast_kernel,
        grid=(n // tn, ni),
        in_specs=[pl.BlockSpec((tm, k), lambda j, i: (i, 0)),
                  pl.BlockSpec((pl.Element(1), pl.Element(tn), pl.Element(k)),
                               lambda j, i: (layer, pl.multiple_of(row0 + j * tn, SUBLANES), 0))]
                 + cast_specs,
        out_specs=[pl.BlockSpec((tm, tn), lambda j, i: (i, j))] + cast_specs,
        out_shape=[jax.ShapeDtypeStruct((m, n), BF16)]
                  + [jax.ShapeDtypeStruct(w.shape, BF16) for w in (wm2, wb2, wo2)],
        scratch_shapes=[pltpu.VMEM((tn, k), BF16)],
        compiler_params=_params("arbitrary", "arbitrary"),
        name=name,
    )(a, w_t, wm2, wb2, wo2)
    return out[0], out[1].reshape(wm.shape), out[2].reshape(wb.shape), out[3].reshape(wo.shape)


def _gmlp_kernel(u_ref, v_ref, z_ref, lng_ref, lnb_ref, ws_ref, bs_ref, o_ref):
    tm, w = v_ref.shape
    gc = w // GMLP_GROUPS
    gv = _gelu(v_ref[...].astype(F32))
    mu = jnp.mean(gv, axis=-1, keepdims=True)
    cen = gv - mu
    var = jnp.mean(cen * cen, axis=-1, keepdims=True)
    sv = (cen * lax.rsqrt(var + EPS) * lng_ref[...] + lnb_ref[...]).astype(BF16)
    for c in range(tm // GMLP_CHUNK):
        rows = slice(c * GMLP_CHUNK, (c + 1) * GMLP_CHUNK)
        for g in range(GMLP_GROUPS):
            cols = slice(g * gc, (g + 1) * gc)
            mixed = _dot(ws_ref[g], sv[rows, cols]) + bs_ref[:, g:g + 1]
            u = _gelu(u_ref[rows, cols].astype(F32))
            z = _silu(z_ref[rows, cols].astype(F32))
            o_ref[rows, cols] = (u * mixed * z).astype(o_ref.dtype)


def _gmlp_call(proj, lng, lnb, ws, bs_t, layer, width):
    m = proj.shape[0]
    tm = min(GMLP_ROWS, m)
    blk = lambda c: pl.BlockSpec((tm, width), lambda i: (i, c))
    full = lambda a: pl.BlockSpec(a.shape, lambda i: (0,) * a.ndim)
    return pl.pallas_call(
        _gmlp_kernel,
        grid=(m // tm,),
        in_specs=[blk(0), blk(1), blk(2), full(lng), full(lnb),
                  pl.BlockSpec((None,) + ws.shape[1:], lambda i: (layer, 0, 0, 0)), full(bs_t)],
        out_specs=pl.BlockSpec((tm, width), lambda i: (i, 0)),
        out_shape=jax.ShapeDtypeStruct((m, width), BF16),
        compiler_params=_params("parallel"),
        name="gmlp",
    )(proj, proj, proj, lng, lnb, ws, bs_t)


def _mm_gmlp_kernel(a_ref, w_ref, u_ref, v_ref, z_ref, lng_ref, lnb_ref, ws_ref, bs_ref,
                    o_ref, za_ref, w_scr):
    _mm_kernel(a_ref, w_ref, o_ref, w_scr)
    _gmlp_kernel(u_ref, v_ref, z_ref, lng_ref, lnb_ref, ws_ref, bs_ref, za_ref)


def _mm_gmlp_rows(m, n):
    tm, tn = _mm_tiles(m, n)
    steps = (n // tn) * (m // tm)
    return m // steps if m % (steps * GMLP_CHUNK) == 0 else 0


def _mm_gmlp_call(a, w_t, layer, row0, n, proj, lng, lnb, ws, bs_t, width, name):
    m, k = a.shape
    tm, tn = _mm_tiles(m, n)
    ni = m // tm
    gr = _mm_gmlp_rows(m, n)
    slab = lambda c: pl.BlockSpec((gr, width), lambda j, i: (j * ni + i, c))
    full = lambda x: pl.BlockSpec(x.shape, lambda j, i: (0,) * x.ndim)
    return pl.pallas_call(
        _mm_gmlp_kernel,
        grid=(n // tn, ni),
        in_specs=[pl.BlockSpec((tm, k), lambda j, i: (i, 0)),
                  pl.BlockSpec((pl.Element(1), pl.Element(tn), pl.Element(k)),
                               lambda j, i: (layer, pl.multiple_of(row0 + j * tn, SUBLANES), 0)),
                  slab(0), slab(1), slab(2), full(lng), full(lnb),
                  pl.BlockSpec((None,) + ws.shape[1:], lambda j, i: (layer, 0, 0, 0)), full(bs_t)],
        out_specs=[pl.BlockSpec((tm, tn), lambda j, i: (i, j)), slab(0)],
        out_shape=[jax.ShapeDtypeStruct((m, n), BF16), jax.ShapeDtypeStruct((m, width), BF16)],
        scratch_shapes=[pltpu.VMEM((tn, k), BF16)],
        compiler_params=_params("arbitrary", "arbitrary"),
        name=name,
    )(a, w_t, proj, proj, proj, lng, lnb, ws, bs_t)


def _gla_constants():
    c, s = GLA_CHUNK, GLA_SUB
    i = np.arange(c)[:, None]
    j = np.arange(c)[None, :]
    same = (i // s) == (j // s)
    mats = []
    for rev in (False, True):
        if not rev:
            cum = same & (j <= i)
            ref = same & ((j % s) <= s // 2 - 1)
        else:
            cum = same & (j >= i)
            ref = same & ((j % s) >= s // 2)
        cum, ref, tot = (t.astype(np.float32) for t in (cum, ref, same))
        blocks = np.concatenate([cum, cum - ref, ref - cum, tot - cum], axis=0)
        mats.append(np.concatenate([blocks, blocks], axis=1))
    return np.stack(mats)


def _gla_kernel(q_ref, k_ref, v_ref, z_ref, lr_ref, wa_ref, ba_ref, gn_ref, cm_ref, o_ref,
                vt_scr, st_scr, o_scr):
    s_len, dk = q_ref.shape
    c, sub = GLA_CHUNK, GLA_SUB
    n_chunks = s_len // c

    vt_scr[...] = v_ref[...].T
    st_scr[...] = jnp.zeros_like(st_scr)

    ri = lax.broadcasted_iota(jnp.int32, (c, c), 0)
    ci = lax.broadcasted_iota(jnp.int32, (c, c), 1)
    same = (ri // sub) == (ci // sub)
    row_in_sub0 = lax.broadcasted_iota(jnp.int32, (c, 1), 0) < sub

    n_grp = min(GLA_GROUP, n_chunks)
    assert n_chunks % n_grp == 0

    def group(d, gi, first):
        rev = d == 1
        cs = range(n_grp)
        base = pl.multiple_of(gi * (n_grp * c), n_grp * c)
        rows = pl.ds(base, n_grp * c)
        crow = [pl.ds(pl.multiple_of(base + i * c, c), c) for i in cs]
        part = lambda x, i: x[i * c:(i + 1) * c]
        q = q_ref[rows, :].astype(F32) * (dk ** -0.5)
        k = k_ref[rows, :].astype(F32)
        g = _log2_sigmoid(_dot(lr_ref[rows, :], wa_ref[d]) + ba_ref[d]) * (1.0 / GLA_TAU)
        g_hi = g.astype(BF16)
        g_lo = (g - g_hi.astype(F32)).astype(BF16)
        e = [_dot(cm_ref[d], jnp.concatenate([part(g_hi, i), part(g_lo, i)], axis=0)) for i in cs]
        cum = [e[i][:c] for i in cs]
        if rev:
            last0, last1 = [x[0:1] for x in cum], [x[sub:sub + 1] for x in cum]
            diag_mask = same & (ci >= ri)
            off_mask = (ri < sub) & (ci >= sub)
        else:
            last0, last1 = [x[sub - 1:sub] for x in cum], [x[c - 1:c] for x in cum]
            diag_mask = same & (ci <= ri)
            off_mask = (ri >= sub) & (ci < sub)
        q_in = [(part(q, i) * jnp.exp2(e[i][c:2 * c])).astype(BF16) for i in cs]
        k_in = [(part(k, i) * jnp.exp2(e[i][2 * c:3 * c])).astype(BF16) for i in cs]
        q_dec = [part(q, i) * jnp.exp2(cum[i]) for i in cs]
        k_dec = [part(k, i) * jnp.exp2(e[i][3 * c:]) for i in cs]
        s_diag = [_dot_nt(q_in[i], k_in[i]) for i in cs]
        s_off = [_dot_nt(q_dec[i].astype(BF16), k_dec[i].astype(BF16)) for i in cs]
        scores = [(jnp.where(diag_mask, s_diag[i], 0.0)
                   + jnp.where(off_mask, s_off[i], 0.0)).astype(BF16) for i in cs]
        o_intra = [_dot(scores[i], v_ref[crow[i], :]) for i in cs]
        if rev:
            q_st = [q_dec[i] * jnp.where(row_in_sub0, jnp.exp2(last1[i]), 1.0) for i in cs]
            k_st = [k_dec[i] * jnp.where(row_in_sub0, 1.0, jnp.exp2(last0[i])) for i in cs]
        else:
            q_st = [q_dec[i] * jnp.where(row_in_sub0, 1.0, jnp.exp2(last0[i])) for i in cs]
            k_st = [k_dec[i] * jnp.where(row_in_sub0, jnp.exp2(last1[i]), 1.0) for i in cs]
        kv_t = [_dot(vt_scr[:, crow[i]], k_st[i].astype(BF16)) for i in cs]
        st = st_scr[d]
        seen = [None] * n_grp
        for i in (reversed(cs) if rev else cs):
            seen[i] = st.astype(BF16)
            st = st * jnp.exp2(last0[i] + last1[i]) + kv_t[i]
        st_scr[d] = st
        for i in cs:
            o = o_intra[i] + _dot_nt(q_st[i].astype(BF16), seen[i])
            if first:
                o_scr[crow[i], :] = o
            else:
                o_scr[crow[i], :] += o

    n_groups = n_chunks // n_grp
    half = n_groups // 2

    def sweep(first):
        def body(t, carry):
            group(0, t, first)
            group(1, n_groups - 1 - t, first)
            return carry
        return body

    if n_groups == 1:
        group(0, 0, True)
        group(1, 0, False)
    elif n_groups % 2 == 0:
        lax.fori_loop(0, half, sweep(True), 0)
        lax.fori_loop(half, n_groups, sweep(False), 0)
    else:
        o_scr[...] = jnp.zeros_like(o_scr)
        lax.fori_loop(0, n_groups, sweep(False), 0)

    o = o_scr[...]
    y = o * lax.rsqrt(jnp.mean(o * o, axis=-1, keepdims=True) + EPS) * gn_ref[...]
    o_ref[...] = (y * _silu(z_ref[...].astype(F32))).astype(o_ref.dtype)


def _gla_call(proj, lr, wa, ba, gnorm, cmats, batch, seq, width, col0):
    m = proj.shape[0]
    h = GLA_HEADS
    dv = width // h
    dk = dv // 2
    qb, kb = col0 // dk, col0 // dk + h
    vb, zb = (col0 + 2 * h * dk) // dv, (col0 + 2 * h * dk) // dv + h
    return pl.pallas_call(
        _gla_kernel,
        grid=(batch, h),
        in_specs=[pl.BlockSpec((seq, dk), lambda b, i: (b, qb + i)),
                  pl.BlockSpec((seq, dk), lambda b, i: (b, kb + i)),
                  pl.BlockSpec((seq, dv), lambda b, i: (b, vb + i)),
                  pl.BlockSpec((seq, dv), lambda b, i: (b, zb + i)),
                  pl.BlockSpec((seq, LANES), lambda b, i: (b, 0)),
                  pl.BlockSpec((2, LANES, dk), lambda b, i: (0, 0, i)),
                  pl.BlockSpec((2, 1, dk), lambda b, i: (0, 0, i)),
                  pl.BlockSpec((1, dv), lambda b, i: (0, 0)),
                  pl.BlockSpec(cmats.shape, lambda b, i: (0, 0, 0))],
        out_specs=pl.BlockSpec((seq, dv), lambda b, i: (b, i)),
        out_shape=jax.ShapeDtypeStruct((m, width), BF16),
        scratch_shapes=[pltpu.VMEM((dv, seq), BF16),
                        pltpu.VMEM((2, dv, dk), F32),
                        pltpu.VMEM((seq, dv), F32)],
        compiler_params=_params("parallel", "parallel"),
        name="gla",
    )(proj, proj, proj, proj, lr, wa, ba, gnorm, cmats)


def _attn_kernel(lam_init, relb_ref, lam_ref, dn_ref, q_ref, k_ref, v_ref, z_ref, o_ref,
                 bias_scr, s_scr, m_scr, vt_scr):
    h = pl.program_id(0)
    s_len, d2 = q_ref.shape
    d = d2 // 2
    tq = bias_scr.shape[1]
    bt = ATTN_BIAS_ROWS
    n_tiles = bias_scr.shape[0] // bt

    @pl.when(pl.program_id(1) == 0)
    def _():
        def tile(t, carry):
            c0 = pl.multiple_of(t * bt, bt)
            rel = (lax.broadcasted_iota(jnp.int32, (bt, tq), 0) + (c0 - (s_len - tq))
                   - lax.broadcasted_iota(jnp.int32, (bt, tq), 1))
            dist = jnp.abs(rel)
            neg = jnp.full((bt, tq), relb_ref[0, h], F32)
            pos = jnp.full((bt, tq), relb_ref[REL_BUCKETS // 2, h], F32)
            for bucket, start in enumerate(T5_STARTS, start=1):
                far = dist >= start
                neg = jnp.where(far, relb_ref[bucket, h], neg)
                pos = jnp.where(far, relb_ref[REL_BUCKETS // 2 + bucket, h], pos)
            bias_scr[pl.ds(c0, bt), :] = jnp.where(rel > 0, pos, neg) * LOG2E
            return carry
        n_buckets = len(T5_STARTS)
        reach = T5_STARTS[-1]
        band_lo = max(0, (s_len - tq - reach) // bt)
        band_hi = min(n_tiles, -(-(s_len + reach) // bt))
        if band_lo > 0:
            bias_scr[:band_lo * bt, :] = jnp.full((band_lo * bt, tq), relb_ref[n_buckets, h] * LOG2E, F32)
        if band_hi < n_tiles:
            bias_scr[band_hi * bt:, :] = jnp.full(
                ((n_tiles - band_hi) * bt, tq), relb_ref[REL_BUCKETS // 2 + n_buckets, h] * LOG2E, F32)
        lax.fori_loop(band_lo, band_hi, tile, 0)

    vt_scr[:d2, :] = v_ref[...].T
    vt_scr[d2:, :] = jnp.ones((vt_scr.shape[0] - d2, s_len), BF16)

    lv = lam_ref[...]
    lam = (jnp.exp(jnp.sum(lv[0:1] * lv[1:2], axis=-1, keepdims=True))
           - jnp.exp(jnp.sum(lv[2:3] * lv[3:4], axis=-1, keepdims=True)) + lam_init)

    n_blocks = s_len // tq

    def logits_stage(i, slot):
        r0 = pl.multiple_of(i * tq, tq)
        bias = bias_scr[pl.ds(pl.multiple_of(s_len - tq - r0, tq), s_len), :]
        q = (q_ref[pl.ds(r0, tq), :].astype(F32) * (d ** -0.5 * LOG2E)).astype(BF16)
        for m in range(2):
            t = _dot_nt(k_ref[:, m * d:(m + 1) * d], q[:, m * d:(m + 1) * d]) + bias
            s_scr[slot, m] = t
            m_scr[slot, m] = jnp.max(t, axis=0, keepdims=True)

    def softmax_stage(i, slot):
        rows = pl.ds(pl.multiple_of(i * tq, tq), tq)
        num, den = [], []
        for m in range(2):
            p = jnp.exp2(s_scr[slot, m] - m_scr[slot, m]).astype(BF16)
            r = _dot(vt_scr[...], p)
            num.append(r[:d2])
            den.append(r[d2:d2 + 1])
        o = (num[0] / den[0] - num[1] * (lam / den[1])).T
        y = o * lax.rsqrt(jnp.mean(o * o, axis=-1, keepdims=True) + EPS) * dn_ref[...]
        y = y * (1.0 - lam_init) * _silu(z_ref[rows, :].astype(F32))
        o_ref[rows, :] = y.astype(o_ref.dtype)

    logits_stage(0, 0)
    if n_blocks % 2 == 0:
        def pair(p, carry):
            logits_stage(2 * p + 1, 1)
            softmax_stage(2 * p, 0)
            logits_stage(2 * p + 2, 0)
            softmax_stage(2 * p + 1, 1)
            return carry
        lax.fori_loop(0, n_blocks // 2 - 1, pair, 0)
        logits_stage(n_blocks - 1, 1)
        softmax_stage(n_blocks - 2, 0)
        softmax_stage(n_blocks - 1, 1)
    else:
        assert n_blocks == 1
        softmax_stage(0, 0)


def _attn_call(proj, rel_bias, lam_par, dnorm, lam_init, batch, seq, width, col0):
    m = proj.shape[0]
    nh = DIFF_HEADS
    hw = width // nh
    c0 = col0 // hw
    tq = min(ATTN_Q_ROWS, seq)
    blk = lambda c: pl.BlockSpec((seq, hw), lambda h, b: (b, c + h))
    return pl.pallas_call(
        functools.partial(_attn_kernel, lam_init),
        grid=(nh, batch),
        in_specs=[pl.BlockSpec(memory_space=pltpu.SMEM),
                  pl.BlockSpec(lam_par.shape, lambda h, b: (0, 0)),
                  pl.BlockSpec((1, hw), lambda h, b: (0, 0)),
                  blk(c0), blk(c0 + nh), blk(c0 + 2 * nh), blk(c0 + 3 * nh)],
        out_specs=pl.BlockSpec((seq, hw), lambda h, b: (b, h)),
        out_shape=jax.ShapeDtypeStruct((m, width), BF16),
        scratch_shapes=[pltpu.VMEM((2 * seq - tq, tq), F32),
                        pltpu.VMEM((2, 2, seq, tq), F32),
                        pltpu.VMEM((2, 2, 1, tq), F32),
                        pltpu.VMEM((hw + BF16_ROWS, seq), BF16)],
        compiler_params=_params("arbitrary", "arbitrary"),
        name="diff_attn",
    )(rel_bias, lam_par, dnorm, proj, proj, proj, proj)


def _merge_kernel(h_ref, za_ref, zb_ref, zc_ref, wm0_ref, wm1_ref, wm2_ref,
                  bm0_ref, bm1_ref, bm2_ref, wb0_ref, wb1_ref, wb2_ref, o_ref):
    hh = h_ref[...]
    acc = None
    for z_ref, wm_ref, bm_ref, wb_ref in ((za_ref, wm0_ref, bm0_ref, wb0_ref),
                                          (zb_ref, wm1_ref, bm1_ref, wb1_ref),
                                          (zc_ref, wm2_ref, bm2_ref, wb2_ref)):
        gate = _sigmoid(_dot(hh, wm_ref[...]) + bm_ref[...])
        term = gate * _dot(z_ref[...], wb_ref[0])
        acc = term if acc is None else acc + term
    o_ref[...] = acc.astype(o_ref.dtype)


def _merge_call(h, za, zb, zc, wm, bm, wb, layer):
    m, d = h.shape
    tm = min(ROW_TILE, m)
    tn = min(MERGE_COLS, d)
    nj = d // tn
    row = pl.BlockSpec((tm, d), lambda j, i: (i, 0))
    wm_spec = lambda br: pl.BlockSpec((None, d, tn), lambda j, i: (layer, 0, br * nj + j))
    bm_spec = lambda br: pl.BlockSpec((None, 1, tn), lambda j, i: (layer, 0, br * nj + j))
    wb_spec = lambda br: pl.BlockSpec((None, 1, d, tn), lambda j, i: (layer, br, 0, j))
    return pl.pallas_call(
        _merge_kernel,
        grid=(nj, m // tm),
        in_specs=[row, row, row, row,
                  wm_spec(0), wm_spec(1), wm_spec(2),
                  bm_spec(0), bm_spec(1), bm_spec(2),
                  wb_spec(0), wb_spec(1), wb_spec(2)],
        out_specs=pl.BlockSpec((tm, tn), lambda j, i: (i, j)),
        out_shape=jax.ShapeDtypeStruct((m, d), BF16),
        compiler_params=_params("arbitrary", "arbitrary"),
        name="merge",
    )(h, za, zb, zc, wm, wm, wm, bm, bm, bm, wb, wb, wb)


def _out_kernel(mg_ref, w_ref, x_ref, g_ref, gn_ref, o_ref, *maybe_h_ref):
    out = _dot(mg_ref[...], w_ref[...])
    y = out * lax.rsqrt(jnp.mean(out * out, axis=-1, keepdims=True) + EPS) * g_ref[...]
    x = x_ref[...] + y
    o_ref[...] = x
    if maybe_h_ref:
        hn = x * lax.rsqrt(jnp.mean(x * x, axis=-1, keepdims=True) + EPS)
        maybe_h_ref[0][...] = (hn * gn_ref[...]).astype(BF16)


def _out_call(merged, w, layer, x2, g, g_next):
    m, d = x2.shape
    tm = min(ROW_TILE, m)
    row = pl.BlockSpec((tm, d), lambda i: (i, 0))
    vec = pl.BlockSpec((1, d), lambda i: (0, 0))
    with_h = g_next is not None
    out = pl.pallas_call(
        _out_kernel,
        grid=(m // tm,),
        in_specs=[row, pl.BlockSpec((None, d, d), lambda i: (layer, 0, 0)), row, vec, vec],
        out_specs=[row, row] if with_h else row,
        out_shape=([jax.ShapeDtypeStruct((m, d), F32), jax.ShapeDtypeStruct((m, d), BF16)]
                   if with_h else jax.ShapeDtypeStruct((m, d), F32)),
        compiler_params=_params("parallel"),
        name="out_proj",
    )(merged, w, x2, g.reshape(1, d), (g_next if with_h else g).reshape(1, d))
    return out if with_h else (out, None)


def kernel(x, norm_pre, w_in, gmlp_ln_g, gmlp_ln_b, gmlp_ws, gmlp_bs, gla_wa2, gla_ba, gla_norm,
           diff_lambda, diff_norm, rel_bias, w_branch, w_merge, b_merge, w_out, norm_post):
    batch, seq, d = x.shape
    depth = norm_pre.shape[0]
    width = d
    m = batch * seq
    hk = gla_wa2.shape[-1]
    lr_col = 3 * width + 2 * hk + 2 * width
    lr_w = 2 * GLA_RANK
    assert seq % GLA_CHUNK == 0 and seq % ATTN_Q_ROWS == 0 and seq % GMLP_CHUNK == 0
    assert w_in.shape[-1] == lr_col + lr_w + 4 * width

    cmats = jnp.asarray(_gla_constants(), BF16)
    w_t = jnp.swapaxes(w_in, 1, 2)
    ws_b = gmlp_ws.astype(BF16)
    bm = b_merge.reshape(depth, 1, -1)
    cast_plan = _mm_cast_plan(m, lr_col, w_merge, w_branch, w_out)
    if cast_plan is None:
        wm_b, wb_b, wo_b = w_merge.astype(BF16), w_branch.astype(BF16), w_out.astype(BF16)

    x2 = x.reshape(m, d)
    h = _rms_call(x2, norm_pre[0])
    for l in range(depth):
        wa = jnp.zeros((2, LANES, hk), F32)
        wa = wa.at[0, :GLA_RANK].set(gla_wa2[l, 0]).at[1, GLA_RANK:lr_w].set(gla_wa2[l, 1]).astype(BF16)
        ba = gla_ba[l].reshape(2, 1, hk)

        if l == 0 and cast_plan is not None:
            proj_ab, wm_b, wb_b, wo_b = _mm_cast_call(h, w_t, l, 0, lr_col, w_merge, w_branch, w_out,
                                                      cast_plan, "in_proj_ab_cast")
        else:
            proj_ab = _mm_call(h, w_t, l, 0, lr_col, "in_proj_ab")
        lr = _mm_call(h, w_t, l, lr_col, LANES, "lr_proj")
        gmlp_args = (proj_ab, gmlp_ln_g[l].reshape(1, width), gmlp_ln_b[l].reshape(1, width),
                     ws_b, gmlp_bs[l].T)
        if _mm_gmlp_rows(m, 4 * width):
            proj_c, za = _mm_gmlp_call(h, w_t, l, lr_col + lr_w, 4 * width, *gmlp_args, width,
                                       "in_proj_c_gmlp")
        else:
            proj_c = _mm_call(h, w_t, l, lr_col + lr_w, 4 * width, "in_proj_c")
            za = _gmlp_call(*gmlp_args, l, width)
        zb = _gla_call(proj_ab, lr, wa, ba, gla_norm[l].reshape(1, -1), cmats, batch, seq, width,
                       3 * width)
        lam_init = 0.8 - 0.6 * math.exp(-0.3 * l)
        zc = _attn_call(proj_c, rel_bias, diff_lambda[l], diff_norm[l].reshape(1, -1), lam_init,
                        batch, seq, width, 0)
        merged = _merge_call(h, za, zb, zc, wm_b, bm, wb_b, l)
        x2, h = _out_call(merged, wo_b, l, x2, norm_post[l],
                          norm_pre[l + 1] if l + 1 < depth else None)
    return x2.reshape(batch, seq, d)
```

```python
import functools
import math

import numpy as np
import jax
import jax.numpy as jnp
from jax import lax
from jax.experimental import pallas as pl
from jax.experimental.pallas import tpu as pltpu

F32 = jnp.float32
BF16 = jnp.bfloat16

EPS = 1e-6
LOG2E = math.log2(math.e)
N_BRANCH = 3
GMLP_CHUNK = 128
GMLP_GROUPS = 8
GLA_HEADS = 4
GLA_RANK = 16
GLA_TAU = 16.0
GLA_SUB = 64
GLA_CHUNK = 2 * GLA_SUB
GLA_GROUP = 8
DIFF_HEADS = 8
ATTN_Q_ROWS = 256
ATTN_BIAS_ROWS = 32
REL_BUCKETS = 32
T5_STARTS = (1, 2, 3, 4, 5, 6, 7, 8, 12, 16, 23, 32, 46, 64, 91)

LANES = 128
SUBLANES = 8
BF16_ROWS = 2 * SUBLANES
VMEM_LIMIT = 52 * 1024 * 1024
MM_TILE = 1024
ROW_TILE = 512
MERGE_COLS = 512
GMLP_ROWS = 256


def _params(*sem):
    return pltpu.CompilerParams(dimension_semantics=sem, vmem_limit_bytes=VMEM_LIMIT)


GELU_C = math.sqrt(2.0 / math.pi)


def _gelu(x):
    hx = 0.5 * x
    return hx + hx * jnp.tanh(x * (GELU_C + (GELU_C * 0.044715) * (x * x)))


def _sigmoid(x):
    return 0.5 + 0.5 * jnp.tanh(0.5 * x)


def _silu(x):
    hx = 0.5 * x
    return hx + hx * jnp.tanh(hx)


def _log2_sigmoid(x):
    u = x * LOG2E
    return jnp.minimum(u, 0.0) - jnp.log2(1.0 + jnp.exp2(-jnp.abs(u)))


def _dot(a, b):
    return jnp.dot(a, b, preferred_element_type=F32)


def _dot_nt(a, b):
    return lax.dot_general(a, b, (((1,), (1,)), ((), ())), preferred_element_type=F32)


def _rms_kernel(x_ref, g_ref, o_ref):
    x = x_ref[...]
    y = x * lax.rsqrt(jnp.mean(x * x, axis=-1, keepdims=True) + EPS)
    o_ref[...] = (y * g_ref[...]).astype(o_ref.dtype)


def _rms_call(x2, g):
    m, d = x2.shape
    tm = min(ROW_TILE, m)
    return pl.pallas_call(
        _rms_kernel,
        grid=(m // tm,),
        in_specs=[pl.BlockSpec((tm, d), lambda i: (i, 0)),
                  pl.BlockSpec((1, d), lambda i: (0, 0))],
        out_specs=pl.BlockSpec((tm, d), lambda i: (i, 0)),
        out_shape=jax.ShapeDtypeStruct((m, d), BF16),
        compiler_params=_params("parallel"),
        name="rms_pre",
    )(x2, g.reshape(1, d))


def _mm_kernel(a_ref, w_ref, o_ref, w_scr):
    @pl.when(pl.program_id(1) == 0)
    def _():
        w_scr[...] = w_ref[0].astype(BF16)
    o_ref[...] = _dot_nt(a_ref[...], w_scr[...]).astype(o_ref.dtype)


def _mm_tiles(m, n):
    return min(MM_TILE, m), min(MM_TILE, n)


def _mm_call(a, w_t, layer, row0, n, name):
    m, k = a.shape
    tm, tn = _mm_tiles(m, n)
    return pl.pallas_call(
        _mm_kernel,
        grid=(n // tn, m // tm),
        in_specs=[pl.BlockSpec((tm, k), lambda j, i: (i, 0)),
                  pl.BlockSpec((pl.Element(1), pl.Element(tn), pl.Element(k)),
                               lambda j, i: (layer, pl.multiple_of(row0 + j * tn, SUBLANES), 0))],
        out_specs=pl.BlockSpec((tm, tn), lambda j, i: (i, j)),
        out_shape=jax.ShapeDtypeStruct((m, n), BF16),
        scratch_shapes=[pltpu.VMEM((tn, k), BF16)],
        compiler_params=_params("arbitrary", "arbitrary"),
        name=name,
    )(a, w_t)


def _mm_cast_kernel(a_ref, w_ref, wm_ref, wb_ref, wo_ref, o_ref, wm_o, wb_o, wo_o, w_scr):
    _mm_kernel(a_ref, w_ref, o_ref, w_scr)
    wm_o[...] = wm_ref[...].astype(BF16)
    wb_o[...] = wb_ref[...].astype(BF16)
    wo_o[...] = wo_ref[...].astype(BF16)


def _mm_cast_plan(m, n, wm, wb, wo):
    tm, tn = _mm_tiles(m, n)
    steps = (n // tn) * (m // tm)
    if steps % N_BRANCH:
        return None
    third = steps // N_BRANCH
    rows_m, rows_b, rows_o = wm.shape[0] * wm.shape[1], math.prod(wb.shape[:3]), wo.shape[0] * wo.shape[1]
    if rows_m % third or rows_b % steps or rows_o % third:
        return None
    plan = (rows_m // third, rows_b // steps, rows_o // third)
    return plan if all(r % BF16_ROWS == 0 for r in plan) else None


def _mm_cast_call(a, w_t, layer, row0, n, wm, wb, wo, plan, name):
    m, k = a.shape
    d = wo.shape[-1]
    tm, tn = _mm_tiles(m, n)
    ni = m // tm
    steps = (n // tn) * ni
    third = steps // N_BRANCH
    rm, rb, ro = plan
    wm2, wb2, wo2 = wm.reshape(-1, N_BRANCH * d), wb.reshape(-1, d), wo.reshape(-1, d)
    step = lambda j, i: j * ni + i
    cast_specs = [pl.BlockSpec((rm, d), lambda j, i: (step(j, i) % third, step(j, i) // third)),
                  pl.BlockSpec((rb, d), lambda j, i: (step(j, i), 0)),
                  pl.BlockSpec((ro, d), lambda j, i: (jnp.minimum(step(j, i), third - 1), 0))]
    out = pl.pallas_call(
        _mm_cast_kernel,
        grid=(n // tn, ni),
        in_specs=[pl.BlockSpec((tm, k), lambda j, i: (i, 0)),
                  pl.BlockSpec((pl.Element(1), pl.Element(tn), pl.Element(k)),
                               lambda j, i: (layer, pl.multiple_of(row0 + j * tn, SUBLANES), 0))]
                 + cast_specs,
        out_specs=[pl.BlockSpec((tm, tn), lambda j, i: (i, j))] + cast_specs,
        out_shape=[jax.ShapeDtypeStruct((m, n), BF16)]
                  + [jax.ShapeDtypeStruct(w.shape, BF16) for w in (wm2, wb2, wo2)],
        scratch_shapes=[pltpu.VMEM((tn, k), BF16)],
        compiler_params=_params("arbitrary", "arbitrary"),
        name=name,
    )(a, w_t, wm2, wb2, wo2)
    return out[0], out[1].reshape(wm.shape), out[2].reshape(wb.shape), out[3].reshape(wo.shape)


def _gmlp_kernel(u_ref, v_ref, z_ref, lng_ref, lnb_ref, ws_ref, bs_ref, o_ref):
    tm, w = v_ref.shape
    gc = w // GMLP_GROUPS
    gv = _gelu(v_ref[...].astype(F32))
    mu = jnp.mean(gv, axis=-1, keepdims=True)
    cen = gv - mu
    var = jnp.mean(cen * cen, axis=-1, keepdims=True)
    sv = (cen * lax.rsqrt(var + EPS) * lng_ref[...] + lnb_ref[...]).astype(BF16)
    for c in range(tm // GMLP_CHUNK):
        rows = slice(c * GMLP_CHUNK, (c + 1) * GMLP_CHUNK)
        for g in range(GMLP_GROUPS):
            cols = slice(g * gc, (g + 1) * gc)
            mixed = _dot(ws_ref[g], sv[rows, cols]) + bs_ref[:, g:g + 1]
            u = _gelu(u_ref[rows, cols].astype(F32))
            z = _silu(z_ref[rows, cols].astype(F32))
            o_ref[rows, cols] = (u * mixed * z).astype(o_ref.dtype)


def _gmlp_call(proj, lng, lnb, ws, bs_t, layer, width):
    m = proj.shape[0]
    tm = min(GMLP_ROWS, m)
    blk = lambda c: pl.BlockSpec((tm, width), lambda i: (i, c))
    full = lambda a: pl.BlockSpec(a.shape, lambda i: (0,) * a.ndim)
    return pl.pallas_call(
        _gmlp_kernel,
        grid=(m // tm,),
        in_specs=[blk(0), blk(1), blk(2), full(lng), full(lnb),
                  pl.BlockSpec((None,) + ws.shape[1:], lambda i: (layer, 0, 0, 0)), full(bs_t)],
        out_specs=pl.BlockSpec((tm, width), lambda i: (i, 0)),
        out_shape=jax.ShapeDtypeStruct((m, width), BF16),
        compiler_params=_params("parallel"),
        name="gmlp",
    )(proj, proj, proj, lng, lnb, ws, bs_t)


def _mm_gmlp_kernel(a_ref, w_ref, u_ref, v_ref, z_ref, lng_ref, lnb_ref, ws_ref, bs_ref,
                    o_ref, za_ref, w_scr):
    _mm_kernel(a_ref, w_ref, o_ref, w_scr)
    _gmlp_kernel(u_ref, v_ref, z_ref, lng_ref, lnb_ref, ws_ref, bs_ref, za_ref)


def _mm_gmlp_rows(m, n):
    tm, tn = _mm_tiles(m, n)
    steps = (n // tn) * (m // tm)
    return m // steps if m % (steps * GMLP_CHUNK) == 0 else 0


def _mm_gmlp_call(a, w_t, layer, row0, n, proj, lng, lnb, ws, bs_t, width, name):
    m, k = a.shape
    tm, tn = _mm_tiles(m, n)
    ni = m // tm
    gr = _mm_gmlp_rows(m, n)
    slab = lambda c: pl.BlockSpec((gr, width), lambda j, i: (j * ni + i, c))
    full = lambda x: pl.BlockSpec(x.shape, lambda j, i: (0,) * x.ndim)
    return pl.pallas_call(
        _mm_gmlp_kernel,
        grid=(n // tn, ni),
        in_specs=[pl.BlockSpec((tm, k), lambda j, i: (i, 0)),
                  pl.BlockSpec((pl.Element(1), pl.Element(tn), pl.Element(k)),
                               lambda j, i: (layer, pl.multiple_of(row0 + j * tn, SUBLANES), 0)),
                  slab(0), slab(1), slab(2), full(lng), full(lnb),
                  pl.BlockSpec((None,) + ws.shape[1:], lambda j, i: (layer, 0, 0, 0)), full(bs_t)],
        out_specs=[pl.BlockSpec((tm, tn), lambda j, i: (i, j)), slab(0)],
        out_shape=[jax.ShapeDtypeStruct((m, n), BF16), jax.ShapeDtypeStruct((m, width), BF16)],
        scratch_shapes=[pltpu.VMEM((tn, k), BF16)],
        compiler_params=_params("arbitrary", "arbitrary"),
        name=name,
    )(a, w_t, proj, proj, proj, lng, lnb, ws, bs_t)


def _gla_constants():
    c, s = GLA_CHUNK, GLA_SUB
    i = np.arange(c)[:, None]
    j = np.arange(c)[None, :]
    same = (i // s) == (j // s)
    mats = []
    for rev in (False, True):
        if not rev:
            cum = same & (j <= i)
            ref = same & ((j % s) <= s // 2 - 1)
        else:
            cum = same & (j >= i)
            ref = same & ((j % s) >= s // 2)
        cum, ref, tot = (t.astype(np.float32) for t in (cum, ref, same))
        blocks = np.concatenate([cum, cum - ref], axis=0)
        mats.append(np.concatenate([blocks, blocks], axis=1))
    return np.stack(mats)


def _gla_kernel(q_ref, k_ref, v_ref, z_ref, lr_ref, wa_ref, ba_ref, gn_ref, cm_ref, o_ref,
                vt_scr, st_scr, o_scr):
    s_len, dk = q_ref.shape
    c, sub = GLA_CHUNK, GLA_SUB
    n_chunks = s_len // c

    vt_scr[...] = v_ref[...].T
    st_scr[...] = jnp.zeros_like(st_scr)

    ri = lax.broadcasted_iota(jnp.int32, (c, c), 0)
    ci = lax.broadcasted_iota(jnp.int32, (c, c), 1)
    same = (ri // sub) == (ci // sub)
    row_in_sub0 = lax.broadcasted_iota(jnp.int32, (c, 1), 0) < sub

    n_grp = min(GLA_GROUP, n_chunks)
    assert n_chunks % n_grp == 0

    def group(d, gi, first):
        rev = d == 1
        cs = range(n_grp)
        base = pl.multiple_of(gi * (n_grp * c), n_grp * c)
        rows = pl.ds(base, n_grp * c)
        crow = [pl.ds(pl.multiple_of(base + i * c, c), c) for i in cs]
        part = lambda x, i: x[i * c:(i + 1) * c]
        q = q_ref[rows, :].astype(F32) * (dk ** -0.5)
        k = k_ref[rows, :].astype(F32)
        g = _log2_sigmoid(_dot(lr_ref[rows, :], wa_ref[d]) + ba_ref[d]) * (1.0 / GLA_TAU)
        g_hi = g.astype(BF16)
        g_lo = (g - g_hi.astype(F32)).astype(BF16)
        e = [_dot(cm_ref[d], jnp.concatenate([part(g_hi, i), part(g_lo, i)], axis=0)) for i in cs]
        cum = [e[i][:c] for i in cs]
        if rev:
            last0, last1 = [x[0:1] for x in cum], [x[sub:sub + 1] for x in cum]
            diag_mask = same & (ci >= ri)
            off_mask = (ri < sub) & (ci >= sub)
        else:
            last0, last1 = [x[sub - 1:sub] for x in cum], [x[c - 1:c] for x in cum]
            diag_mask = same & (ci <= ri)
            off_mask = (ri >= sub) & (ci < sub)
        q_in = [(part(q, i) * jnp.exp2(e[i][c:2 * c])).astype(BF16) for i in cs]
        k_in = [(part(k, i) * jnp.exp2(-e[i][c:2 * c])).astype(BF16) for i in cs]
        q_dec = [part(q, i) * jnp.exp2(cum[i]) for i in cs]
        k_dec = [part(k, i) * jnp.exp2(jnp.where(row_in_sub0, last0[i], last1[i]) - cum[i]) for i in cs]
        s_diag = [_dot_nt(q_in[i], k_in[i]) for i in cs]
        s_off = [_dot_nt(q_dec[i].astype(BF16), k_dec[i].astype(BF16)) for i in cs]
        scores = [(jnp.where(diag_mask, s_diag[i], 0.0)
                   + jnp.where(off_mask, s_off[i], 0.0)).astype(BF16) for i in cs]
        o_intra = [_dot(scores[i], v_ref[crow[i], :]) for i in cs]
        if rev:
            q_st = [q_dec[i] * jnp.where(row_in_sub0, jnp.exp2(last1[i]), 1.0) for i in cs]
            k_st = [k_dec[i] * jnp.where(row_in_sub0, 1.0, jnp.exp2(last0[i])) for i in cs]
        else:
            q_st = [q_dec[i] * jnp.where(row_in_sub0, 1.0, jnp.exp2(last0[i])) for i in cs]
            k_st = [k_dec[i] * jnp.where(row_in_sub0, jnp.exp2(last1[i]), 1.0) for i in cs]
        kv_t = [_dot(vt_scr[:, crow[i]], k_st[i].astype(BF16)) for i in cs]
        st = st_scr[d]
        seen = [None] * n_grp
        for i in (reversed(cs) if rev else cs):
            seen[i] = st.astype(BF16)
            st = st * jnp.exp2(last0[i] + last1[i]) + kv_t[i]
        st_scr[d] = st
        for i in cs:
            o = o_intra[i] + _dot_nt(q_st[i].astype(BF16), seen[i])
            if first:
                o_scr[crow[i], :] = o
            else:
                o_scr[crow[i], :] += o

    n_groups = n_chunks // n_grp
    half = n_groups // 2

    def sweep(first):
        def body(t, carry):
            group(0, t, first)
            group(1, n_groups - 1 - t, first)
            return carry
        return body

    if n_groups == 1:
        group(0, 0, True)
        group(1, 0, False)
    elif n_groups % 2 == 0:
        lax.fori_loop(0, half, sweep(True), 0)
        lax.fori_loop(half, n_groups, sweep(False), 0)
    else:
        o_scr[...] = jnp.zeros_like(o_scr)
        lax.fori_loop(0, n_groups, sweep(False), 0)

    o = o_scr[...]
    y = o * lax.rsqrt(jnp.mean(o * o, axis=-1, keepdims=True) + EPS) * gn_ref[...]
    o_ref[...] = (y * _silu(z_ref[...].astype(F32))).astype(o_ref.dtype)


def _gla_call(proj, lr, wa, ba, gnorm, cmats, batch, seq, width, col0):
    m = proj.shape[0]
    h = GLA_HEADS
    dv = width // h
    dk = dv // 2
    qb, kb = col0 // dk, col0 // dk + h
    vb, zb = (col0 + 2 * h * dk) // dv, (col0 + 2 * h * dk) // dv + h
    return pl.pallas_call(
        _gla_kernel,
        grid=(batch, h),
        in_specs=[pl.BlockSpec((seq, dk), lambda b, i: (b, qb + i)),
                  pl.BlockSpec((seq, dk), lambda b, i: (b, kb + i)),
                  pl.BlockSpec((seq, dv), lambda b, i: (b, vb + i)),
                  pl.BlockSpec((seq, dv), lambda b, i: (b, zb + i)),
                  pl.BlockSpec((seq, LANES), lambda b, i: (b, 0)),
                  pl.BlockSpec((2, LANES, dk), lambda b, i: (0, 0, i)),
                  pl.BlockSpec((2, 1, dk), lambda b, i: (0, 0, i)),
                  pl.BlockSpec((1, dv), lambda b, i: (0, 0)),
                  pl.BlockSpec(cmats.shape, lambda b, i: (0, 0, 0))],
        out_specs=pl.BlockSpec((seq, dv), lambda b, i: (b, i)),
        out_shape=jax.ShapeDtypeStruct((m, width), BF16),
        scratch_shapes=[pltpu.VMEM((dv, seq), BF16),
                        pltpu.VMEM((2, dv, dk), F32),
                        pltpu.VMEM((seq, dv), F32)],
        compiler_params=_params("parallel", "parallel"),
        name="gla",
    )(proj, proj, proj, proj, lr, wa, ba, gnorm, cmats)


def _attn_kernel(lam_init, relb_ref, lam_ref, dn_ref, q_ref, k_ref, v_ref, z_ref, o_ref,
                 bias_scr, s_scr, m_scr, vt_scr):
    h = pl.program_id(0)
    s_len, d2 = q_ref.shape
    d = d2 // 2
    tq = bias_scr.shape[1]
    bt = ATTN_BIAS_ROWS
    n_tiles = bias_scr.shape[0] // bt

    @pl.when(pl.program_id(1) == 0)
    def _():
        def tile(t, carry):
            c0 = pl.multiple_of(t * bt, bt)
            rel = (lax.broadcasted_iota(jnp.int32, (bt, tq), 0) + (c0 - (s_len - tq))
                   - lax.broadcasted_iota(jnp.int32, (bt, tq), 1))
            dist = jnp.abs(rel)
            neg = jnp.full((bt, tq), relb_ref[0, h], F32)
            pos = jnp.full((bt, tq), relb_ref[REL_BUCKETS // 2, h], F32)
            for bucket, start in enumerate(T5_STARTS, start=1):
                far = dist >= start
                neg = jnp.where(far, relb_ref[bucket, h], neg)
                pos = jnp.where(far, relb_ref[REL_BUCKETS // 2 + bucket, h], pos)
            bias_scr[pl.ds(c0, bt), :] = jnp.where(rel > 0, pos, neg) * LOG2E
            return carry
        n_buckets = len(T5_STARTS)
        reach = T5_STARTS[-1]
        band_lo = max(0, (s_len - tq - reach) // bt)
        band_hi = min(n_tiles, -(-(s_len + reach) // bt))
        if band_lo > 0:
            bias_scr[:band_lo * bt, :] = jnp.full((band_lo * bt, tq), relb_ref[n_buckets, h] * LOG2E, F32)
        if band_hi < n_tiles:
            bias_scr[band_hi * bt:, :] = jnp.full(
                ((n_tiles - band_hi) * bt, tq), relb_ref[REL_BUCKETS // 2 + n_buckets, h] * LOG2E, F32)
        lax.fori_loop(band_lo, band_hi, tile, 0)

    vt_scr[:d2, :] = v_ref[...].T
    vt_scr[d2:, :] = jnp.ones((vt_scr.shape[0] - d2, s_len), BF16)

    lv = lam_ref[...]
    lam = (jnp.exp(jnp.sum(lv[0:1] * lv[1:2], axis=-1, keepdims=True))
           - jnp.exp(jnp.sum(lv[2:3] * lv[3:4], axis=-1, keepdims=True)) + lam_init)

    n_blocks = s_len // tq

    def logits_stage(i, slot):
        r0 = pl.multiple_of(i * tq, tq)
        bias = bias_scr[pl.ds(pl.multiple_of(s_len - tq - r0, tq), s_len), :]
        q = (q_ref[pl.ds(r0, tq), :].astype(F32) * (d ** -0.5 * LOG2E)).astype(BF16)
        for m in range(2):
            t = _dot_nt(k_ref[:, m * d:(m + 1) * d], q[:, m * d:(m + 1) * d]) + bias
            s_scr[slot, m] = t
            m_scr[slot, m] = jnp.max(t, axis=0, keepdims=True)

    def softmax_stage(i, slot):
        rows = pl.ds(pl.multiple_of(i * tq, tq), tq)
        num, den = [], []
        for m in range(2):
            p = jnp.exp2(s_scr[slot, m] - m_scr[slot, m]).astype(BF16)
            r = _dot(vt_scr[...], p)
            num.append(r[:d2])
            den.append(r[d2:d2 + 1])
        o = (num[0] / den[0] - num[1] * (lam / den[1])).T
        y = o * lax.rsqrt(jnp.mean(o * o, axis=-1, keepdims=True) + EPS) * dn_ref[...]
        y = y * (1.0 - lam_init) * _silu(z_ref[rows, :].astype(F32))
        o_ref[rows, :] = y.astype(o_ref.dtype)

    logits_stage(0, 0)
    if n_blocks % 2 == 0:
        def pair(p, carry):
            logits_stage(2 * p + 1, 1)
            softmax_stage(2 * p, 0)
            logits_stage(2 * p + 2, 0)
            softmax_stage(2 * p + 1, 1)
            return carry
        lax.fori_loop(0, n_blocks // 2 - 1, pair, 0)
        logits_stage(n_blocks - 1, 1)
        softmax_stage(n_blocks - 2, 0)
        softmax_stage(n_blocks - 1, 1)
    else:
        assert n_blocks == 1
        softmax_stage(0, 0)


def _attn_call(proj, rel_bias, lam_par, dnorm, lam_init, batch, seq, width, col0):
    m = proj.shape[0]
    nh = DIFF_HEADS
    hw = width // nh
    c0 = col0 // hw
    tq = min(ATTN_Q_ROWS, seq)
    blk = lambda c: pl.BlockSpec((seq, hw), lambda h, b: (b, c + h))
    return pl.pallas_call(
        functools.partial(_attn_kernel, lam_init),
        grid=(nh, batch),
        in_specs=[pl.BlockSpec(memory_space=pltpu.SMEM),
                  pl.BlockSpec(lam_par.shape, lambda h, b: (0, 0)),
                  pl.BlockSpec((1, hw), lambda h, b: (0, 0)),
                  blk(c0), blk(c0 + nh), blk(c0 + 2 * nh), blk(c0 + 3 * nh)],
        out_specs=pl.BlockSpec((seq, hw), lambda h, b: (b, h)),
        out_shape=jax.ShapeDtypeStruct((m, width), BF16),
        scratch_shapes=[pltpu.VMEM((2 * seq - tq, tq), F32),
                        pltpu.VMEM((2, 2, seq, tq), F32),
                        pltpu.VMEM((2, 2, 1, tq), F32),
                        pltpu.VMEM((hw + BF16_ROWS, seq), BF16)],
        compiler_params=_params("arbitrary", "arbitrary"),
        name="diff_attn",
    )(rel_bias, lam_par, dnorm, proj, proj, proj, proj)


def _merge_kernel(h_ref, za_ref, zb_ref, zc_ref, wm0_ref, wm1_ref, wm2_ref,
                  bm0_ref, bm1_ref, bm2_ref, wb0_ref, wb1_ref, wb2_ref, o_ref):
    hh = h_ref[...]
    acc = None
    for z_ref, wm_ref, bm_ref, wb_ref in ((za_ref, wm0_ref, bm0_ref, wb0_ref),
                                          (zb_ref, wm1_ref, bm1_ref, wb1_ref),
                                          (zc_ref, wm2_ref, bm2_ref, wb2_ref)):
        gate = _sigmoid(_dot(hh, wm_ref[...]) + bm_ref[...])
        term = gate * _dot(z_ref[...], wb_ref[0])
        acc = term if acc is None else acc + term
    o_ref[...] = acc.astype(o_ref.dtype)


def _merge_call(h, za, zb, zc, wm, bm, wb, layer):
    m, d = h.shape
    tm = min(ROW_TILE, m)
    tn = min(MERGE_COLS, d)
    nj = d // tn
    row = pl.BlockSpec((tm, d), lambda j, i: (i, 0))
    wm_spec = lambda br: pl.BlockSpec((None, d, tn), lambda j, i: (layer, 0, br * nj + j))
    bm_spec = lambda br: pl.BlockSpec((None, 1, tn), lambda j, i: (layer, 0, br * nj + j))
    wb_spec = lambda br: pl.BlockSpec((None, 1, d, tn), lambda j, i: (layer, br, 0, j))
    return pl.pallas_call(
        _merge_kernel,
        grid=(nj, m // tm),
        in_specs=[row, row, row, row,
                  wm_spec(0), wm_spec(1), wm_spec(2),
                  bm_spec(0), bm_spec(1), bm_spec(2),
                  wb_spec(0), wb_spec(1), wb_spec(2)],
        out_specs=pl.BlockSpec((tm, tn), lambda j, i: (i, j)),
        out_shape=jax.ShapeDtypeStruct((m, d), BF16),
        compiler_params=_params("arbitrary", "arbitrary"),
        name="merge",
    )(h, za, zb, zc, wm, wm, wm, bm, bm, bm, wb, wb, wb)


def _out_kernel(mg_ref, w_ref, x_ref, g_ref, gn_ref, o_ref, *maybe_h_ref):
    out = _dot(mg_ref[...], w_ref[...])
    y = out * lax.rsqrt(jnp.mean(out * out, axis=-1, keepdims=True) + EPS) * g_ref[...]
    x = x_ref[...] + y
    o_ref[...] = x
    if maybe_h_ref:
        hn = x * lax.rsqrt(jnp.mean(x * x, axis=-1, keepdims=True) + EPS)
        maybe_h_ref[0][...] = (hn * gn_ref[...]).astype(BF16)


def _out_call(merged, w, layer, x2, g, g_next):
    m, d = x2.shape
    tm = min(ROW_TILE, m)
    row = pl.BlockSpec((tm, d), lambda i: (i, 0))
    vec = pl.BlockSpec((1, d), lambda i: (0, 0))
    with_h = g_next is not None
    out = pl.pallas_call(
        _out_kernel,
        grid=(m // tm,),
        in_specs=[row, pl.BlockSpec((None, d, d), lambda i: (layer, 0, 0)), row, vec, vec],
        out_specs=[row, row] if with_h else row,
        out_shape=([jax.ShapeDtypeStruct((m, d), F32), jax.ShapeDtypeStruct((m, d), BF16)]
                   if with_h else jax.ShapeDtypeStruct((m, d), F32)),
        compiler_params=_params("parallel"),
        name="out_proj",
    )(merged, w, x2, g.reshape(1, d), (g_next if with_h else g).reshape(1, d))
    return out if with_h else (out, None)


def kernel(x, norm_pre, w_in, gmlp_ln_g, gmlp_ln_b, gmlp_ws, gmlp_bs, gla_wa2, gla_ba, gla_norm,
           diff_lambda, diff_norm, rel_bias, w_branch, w_merge, b_merge, w_out, norm_post):
    batch, seq, d = x.shape
    depth = norm_pre.shape[0]
    width = d
    m = batch * seq
    hk = gla_wa2.shape[-1]
    lr_col = 3 * width + 2 * hk + 2 * width
    lr_w = 2 * GLA_RANK
    assert seq % GLA_CHUNK == 0 and seq % ATTN_Q_ROWS == 0 and seq % GMLP_CHUNK == 0
    assert w_in.shape[-1] == lr_col + lr_w + 4 * width

    cmats = jnp.asarray(_gla_constants(), BF16)
    w_t = jnp.swapaxes(w_in, 1, 2)
    ws_b = gmlp_ws.astype(BF16)
    bm = b_merge.reshape(depth, 1, -1)
    cast_plan = _mm_cast_plan(m, lr_col, w_merge, w_branch, w_out)
    if cast_plan is None:
        wm_b, wb_b, wo_b = w_merge.astype(BF16), w_branch.astype(BF16), w_out.astype(BF16)

    x2 = x.reshape(m, d)
    h = _rms_call(x2, norm_pre[0])
    for l in range(depth):
        wa = jnp.zeros((2, LANES, hk), F32)
        wa = wa.at[0, :GLA_RANK].set(gla_wa2[l, 0]).at[1, GLA_RANK:lr_w].set(gla_wa2[l, 1]).astype(BF16)
        ba = gla_ba[l].reshape(2, 1, hk)

        if l == 0 and cast_plan is not None:
            proj_ab, wm_b, wb_b, wo_b = _mm_cast_call(h, w_t, l, 0, lr_col, w_merge, w_branch, w_out,
                                                      cast_plan, "in_proj_ab_cast")
        else:
            proj_ab = _mm_call(h, w_t, l, 0, lr_col, "in_proj_ab")
        lr = _mm_call(h, w_t, l, lr_col, LANES, "lr_proj")
        gmlp_args = (proj_ab, gmlp_ln_g[l].reshape(1, width), gmlp_ln_b[l].reshape(1, width),
                     ws_b, gmlp_bs[l].T)
        if _mm_gmlp_rows(m, 4 * width):
            proj_c, za = _mm_gmlp_call(h, w_t, l, lr_col + lr_w, 4 * width, *gmlp_args, width,
                                       "in_proj_c_gmlp")
        else:
            proj_c = _mm_call(h, w_t, l, lr_col + lr_w, 4 * width, "in_proj_c")
            za = _gmlp_call(*gmlp_args, l, width)
        zb = _gla_call(proj_ab, lr, wa, ba, gla_norm[l].reshape(1, -1), cmats, batch, seq, width,
                       3 * width)
        lam_init = 0.8 - 0.6 * math.exp(-0.3 * l)
        zc = _attn_call(proj_c, rel_bias, diff_lambda[l], diff_norm[l].reshape(1, -1), lam_init,
                        batch, seq, width, 0)
        merged = _merge_call(h, za, zb, zc, wm_b, bm, wb_b, l)
        x2, h = _out_call(merged, wo_b, l, x2, norm_post[l],
                          norm_pre[l + 1] if l + 1 < depth else None)
    return x2.reshape(batch, seq, d)
```
